```python
import numpy as np
import jax
import jax.numpy as jnp
from jax import lax

D_MODEL = 1024
BATCH = 8
SEQ = 2048
DEPTH = 1
DEC_BATCH = 128
DEC_SEQ = 8
PAST_LEN = 8192
PAGE_SIZE = 128

MLA_HEADS = 8
MLA_NOPE = 64
MLA_ROPE = 32
MLA_V = 64
Q_LORA = 768
KV_LORA = 256
ROPE_THETA = 10000.0
MLA_QBLOCK = 128
MLA_SCALE = (MLA_NOPE + MLA_ROPE) ** -0.5

MOBA_HEADS = 8
MOBA_DH = 64
MOBA_WIDTH = MOBA_HEADS * MOBA_DH
MOBA_BLOCK = 256
MOBA_TOPK = 3
MOBA_QBLOCK = 16
MOBA_SCALE = MOBA_DH ** -0.5

PEER_HEADS = 8
PEER_NKEYS = 128
PEER_EXPERTS = PEER_NKEYS * PEER_NKEYS
PEER_DKEY = 256
PEER_TOPK = 16
PEER_CHUNK = 128

IN_WIDTH = Q_LORA + KV_LORA + MLA_ROPE + 3 * MOBA_WIDTH + 2 * D_MODEL
RMS_EPS = 1e-6
NEG_INF = -1e30

kernel_name = 'mla_moba_peer_gated_hybrid_step'


def _in_splits():
    sizes = (Q_LORA, KV_LORA, MLA_ROPE, MOBA_WIDTH, MOBA_WIDTH, MOBA_WIDTH, D_MODEL, D_MODEL)
    return [int(v) for v in np.cumsum(sizes)[:-1]]


def _rmsnorm(x, w):
    xf = x.astype(jnp.float32)
    y = xf * lax.rsqrt(jnp.mean(xf * xf, axis=-1, keepdims=True) + RMS_EPS)
    return (y * w.astype(jnp.float32)).astype(x.dtype)


def _rope(x, pos):
    half = MLA_ROPE // 2
    inv = ROPE_THETA ** (-jnp.arange(half, dtype=jnp.float32) / half)
    ang = pos.astype(jnp.float32)[..., None] * inv
    cos, sin = jnp.cos(ang), jnp.sin(ang)
    x1 = x[..., :half].astype(jnp.float32)
    x2 = x[..., half:].astype(jnp.float32)
    return jnp.concatenate([x1 * cos - x2 * sin, x2 * cos + x1 * sin], axis=-1).astype(x.dtype)


def _alibi_slopes():
    return 2.0 ** (-8.0 * jnp.arange(1, MOBA_HEADS + 1, dtype=jnp.float32) / MOBA_HEADS)


def _mixer_inputs(x, pos, attn_norm, w_in, q_a_norm, w_q_b, kv_a_norm, w_uk):
    B, T, _ = x.shape
    xn = _rmsnorm(x, attn_norm)
    z = xn @ w_in
    q_a, c_kv, k_r, q_m, k_m, v_m, g_a, g_b = jnp.split(z, _in_splits(), axis=-1)
    q = (_rmsnorm(q_a, q_a_norm) @ w_q_b).reshape(B, T, MLA_HEADS, MLA_NOPE + MLA_ROPE)
    q_lat = jnp.einsum('bthn,chn->bthc', q[..., :MLA_NOPE], w_uk)
    q_rope = _rope(q[..., MLA_NOPE:], pos[:, None])
    c_kv = _rmsnorm(c_kv, kv_a_norm)
    k_r = _rope(k_r, pos)
    shp = (B, T, MOBA_HEADS, MOBA_DH)
    return (q_lat, q_rope, c_kv, k_r, q_m.reshape(shp), k_m.reshape(shp), v_m.reshape(shp), g_a, g_b)


def _mla_scores(q_lat, q_rope, c, kr):
    s = jnp.einsum('bqhc,bkc->bhqk', q_lat, c) + jnp.einsum('bqhr,bkr->bhqk', q_rope, kr)
    return s.astype(jnp.float32) * MLA_SCALE


def _mla_prompt(q_lat, q_rope, c, kr):
    B, S = c.shape[:2]
    nqb = S // MLA_QBLOCK
    ql = jnp.moveaxis(q_lat.reshape(B, nqb, MLA_QBLOCK, MLA_HEADS, KV_LORA), 1, 0)
    qr = jnp.moveaxis(q_rope.reshape(B, nqb, MLA_QBLOCK, MLA_HEADS, MLA_ROPE), 1, 0)
    kpos = jnp.arange(S)

    def block(args):
        i, qlb, qrb = args
        qpos = i * MLA_QBLOCK + jnp.arange(MLA_QBLOCK)
        s = jnp.where(kpos[None, :] <= qpos[:, None], _mla_scores(qlb, qrb, c, kr), NEG_INF)
        p = jax.nn.softmax(s, axis=-1).astype(c.dtype)
        return jnp.einsum('bhqk,bkc->bqhc', p, c)

    o = lax.map(block, (jnp.arange(nqb), ql, qr))
    return jnp.moveaxis(o, 0, 1).reshape(B, S, MLA_HEADS, KV_LORA)


def _online_update(carry, s, v):
    m, l, acc = carry
    m_new = jnp.maximum(m, s.max(-1))
    alpha = jnp.exp(m - m_new)
    p = jnp.exp(s - m_new[..., None])
    l = l * alpha + p.sum(-1)
    acc = acc * alpha[..., None] + jnp.einsum('bhqk,bkc->bhqc', p, v.astype(jnp.float32))
    return (m_new, l, acc)


def _mla_sample(q_lat, q_rope, c_new, kr_new, cache_c, cache_r, page_table, layer):
    Bd, T = c_new.shape[:2]

    def page_step(carry, phys):
        cp = cache_c[layer, phys]
        rp = cache_r[layer, phys]
        return _online_update(carry, _mla_scores(q_lat, q_rope, cp, rp), cp), None

    init = (jnp.full((Bd, MLA_HEADS, T), NEG_INF, jnp.float32),
            jnp.zeros((Bd, MLA_HEADS, T), jnp.float32),
            jnp.zeros((Bd, MLA_HEADS, T, KV_LORA), jnp.float32))
    carry, _ = lax.scan(page_step, init, page_table.T)
    causal = jnp.arange(T)[None, :] <= jnp.arange(T)[:, None]
    s_new = jnp.where(causal, _mla_scores(q_lat, q_rope, c_new, kr_new), NEG_INF)
    _, l, acc = _online_update(carry, s_new, c_new)
    o = acc / l[..., None]
    return jnp.swapaxes(o, 1, 2).astype(c_new.dtype)


def _merge(o_lat, o_moba, g_a, g_b, w_uv, w_up_a, w_up_b, w_o):
    B, T = o_moba.shape[:2]
    y_a = jnp.einsum('bthc,chv->bthv', o_lat, w_uv).reshape(B, T, MLA_HEADS * MLA_V) @ w_up_a
    y_b = o_moba.reshape(B, T, MOBA_WIDTH) @ w_up_b
    return (jax.nn.sigmoid(g_a) * y_a + jax.nn.sigmoid(g_b) * y_b) @ w_o


def _select_blocks(q, kmean, n_past):
    nb = kmean.shape[1]
    s = jnp.einsum('bqhd,bnhd->bqhn', q, kmean).astype(jnp.float32)
    s = jnp.where(jnp.arange(nb) < n_past, s, NEG_INF)
    _, idx = lax.top_k(s, MOBA_TOPK)
    valid = jnp.arange(MOBA_TOPK) < n_past
    return idx, valid


def _moba_attend(q, qpos, k_sel, v_sel, pos_sel, valid_sel, k_own, v_own, pos_own):
    slopes = _alibi_slopes()
    s_sel = jnp.einsum('bqhd,bqhnd->bqhn', q, k_sel).astype(jnp.float32) * MOBA_SCALE
    s_sel = s_sel - slopes[:, None] * (qpos[:, None, None] - pos_sel).astype(jnp.float32)
    s_sel = jnp.where(valid_sel, s_sel, NEG_INF)
    dist = qpos[:, None] - pos_own[None, :]
    s_own = jnp.einsum('bqhd,blhd->bqhl', q, k_own).astype(jnp.float32) * MOBA_SCALE
    s_own = s_own - slopes[None, :, None] * dist[:, None, :].astype(jnp.float32)
    s_own = jnp.where((dist >= 0)[:, None, :], s_own, NEG_INF)
    p = jax.nn.softmax(jnp.concatenate([s_sel, s_own], axis=-1), axis=-1).astype(v_own.dtype)
    n = s_sel.shape[-1]
    return (jnp.einsum('bqhn,bqhnd->bqhd', p[..., :n], v_sel)
            + jnp.einsum('bqhl,blhd->bqhd', p[..., n:], v_own))


def _moba_prompt(q, k, v):
    B, S = q.shape[:2]
    nb = max(-(-S // MOBA_BLOCK), MOBA_TOPK)
    pad = nb * MOBA_BLOCK - S
    kp = jnp.pad(k, ((0, 0), (0, pad), (0, 0), (0, 0)))
    vp = jnp.pad(v, ((0, 0), (0, pad), (0, 0), (0, 0)))
    kb = kp.reshape(B, nb, MOBA_BLOCK, MOBA_HEADS, MOBA_DH)
    vb = vp.reshape(B, nb, MOBA_BLOCK, MOBA_HEADS, MOBA_DH)
    kmean = jnp.mean(kb, axis=2, dtype=jnp.float32)
    nqb = S // MOBA_QBLOCK
    qs = jnp.moveaxis(q.reshape(B, nqb, MOBA_QBLOCK, MOBA_HEADS, MOBA_DH), 1, 0)
    bidx = jnp.arange(B)[:, None, None, None]
    hidx = jnp.arange(MOBA_HEADS)[None, None, :, None]
    n_sel = MOBA_TOPK * MOBA_BLOCK
    valid_rows = jnp.repeat(jnp.arange(MOBA_TOPK), MOBA_BLOCK)

    def block(args):
        i, qb = args
        qpos = i * MOBA_QBLOCK + jnp.arange(MOBA_QBLOCK)
        own = (i * MOBA_QBLOCK) // MOBA_BLOCK
        idx, valid = _select_blocks(qb, kmean, own)
        shp = (B, MOBA_QBLOCK, MOBA_HEADS, n_sel)
        k_sel = kb[bidx, idx, :, hidx].reshape(shp + (MOBA_DH,))
        v_sel = vb[bidx, idx, :, hidx].reshape(shp + (MOBA_DH,))
        pos_sel = (idx[..., None] * MOBA_BLOCK + jnp.arange(MOBA_BLOCK)).reshape(shp)
        k_own = lax.dynamic_slice_in_dim(kp, own * MOBA_BLOCK, MOBA_BLOCK, axis=1)
        v_own = lax.dynamic_slice_in_dim(vp, own * MOBA_BLOCK, MOBA_BLOCK, axis=1)
        pos_own = own * MOBA_BLOCK + jnp.arange(MOBA_BLOCK)
        return _moba_attend(qb, qpos, k_sel, v_sel, pos_sel, valid[valid_rows], k_own, v_own, pos_own)

    o = lax.map(block, (jnp.arange(nqb), qs))
    return jnp.moveaxis(o, 0, 1).reshape(B, S, MOBA_HEADS, MOBA_DH)


def _moba_sample(q, k_new, v_new, cache_k, cache_v, page_means, page_table, layer):
    Bd, T = q.shape[:2]
    n_pages = page_table.shape[1]
    ppb = MOBA_BLOCK // PAGE_SIZE
    past = n_pages * PAGE_SIZE
    n_full = past // MOBA_BLOCK
    rem = past - n_full * MOBA_BLOCK
    nb = max(n_full, MOBA_TOPK)
    full_pages = page_table[:, :n_full * ppb]
    kmean = page_means[full_pages].reshape(Bd, n_full, ppb, MOBA_HEADS, MOBA_DH).mean(axis=2)
    kmean = jnp.pad(kmean, ((0, 0), (0, nb - n_full), (0, 0), (0, 0)))
    idx, valid = _select_blocks(q, kmean, n_full)
    logical = jnp.minimum(idx[..., None] * ppb + jnp.arange(ppb), n_pages - 1)
    phys = page_table[jnp.arange(Bd)[:, None, None, None, None], logical]
    n_sel = MOBA_TOPK * MOBA_BLOCK
    pos_sel = (idx[..., None] * MOBA_BLOCK + jnp.arange(MOBA_BLOCK)).reshape(Bd, T, MOBA_HEADS, n_sel)
    valid_sel = valid[jnp.repeat(jnp.arange(MOBA_TOPK), MOBA_BLOCK)]
    own_pages = page_table[:, n_full * ppb:]
    k_own = jnp.concatenate([cache_k[layer, own_pages].reshape(Bd, rem, MOBA_HEADS, MOBA_DH), k_new], axis=1)
    v_own = jnp.concatenate([cache_v[layer, own_pages].reshape(Bd, rem, MOBA_HEADS, MOBA_DH), v_new], axis=1)
    pos_own = n_full * MOBA_BLOCK + jnp.arange(rem + T)
    hidx = jnp.arange(MOBA_HEADS)[None, :, None, None]

    def token(args):
        q_t, phys_t, pos_t, qpos_t = args
        k_sel = cache_k[layer, phys_t, :, hidx].reshape(Bd, 1, MOBA_HEADS, n_sel, MOBA_DH)
        v_sel = cache_v[layer, phys_t, :, hidx].reshape(Bd, 1, MOBA_HEADS, n_sel, MOBA_DH)
        o = _moba_attend(q_t[:, None], qpos_t[None], k_sel, v_sel, pos_t[:, None], valid_sel,
                         k_own, v_own, pos_own)
        return o[:, 0]

    o = lax.map(token, (jnp.moveaxis(q, 1, 0), jnp.moveaxis(phys, 1, 0), jnp.moveaxis(pos_sel, 1, 0),
                        past + jnp.arange(T)))
    return jnp.moveaxis(o, 0, 1)


def _peer(x, w_query, keys_1, keys_2, expert_u, expert_v):
    B, T, D = x.shape
    n = B * T
    pad = (-n) % PEER_CHUNK
    xt = jnp.pad(x.reshape(n, D), ((0, pad), (0, 0))).reshape(-1, PEER_CHUNK, D)
    half = PEER_DKEY // 2
    n_cand = PEER_TOPK * PEER_TOPK

    def chunk(xc):
        q = (xc @ w_query).reshape(PEER_CHUNK, PEER_HEADS, PEER_DKEY)
        s1 = jnp.einsum('chd,hkd->chk', q[..., :half], keys_1).astype(jnp.float32)
        s2 = jnp.einsum('chd,hkd->chk', q[..., half:], keys_2).astype(jnp.float32)
        v1, i1 = lax.top_k(s1, PEER_TOPK)
        v2, i2 = lax.top_k(s2, PEER_TOPK)
        cand_s = (v1[..., :, None] + v2[..., None, :]).reshape(PEER_CHUNK, PEER_HEADS, n_cand)
        cand_e = (i1[..., :, None] * PEER_NKEYS + i2[..., None, :]).reshape(PEER_CHUNK, PEER_HEADS, n_cand)
        top_s, top_pos = lax.top_k(cand_s, PEER_TOPK)
        e = jnp.take_along_axis(cand_e, top_pos, axis=-1)
        g = jax.nn.softmax(top_s, axis=-1)
        act = jax.nn.gelu(jnp.einsum('cd,chkd->chk', xc, expert_u[e]).astype(jnp.float32), approximate=False)
        return jnp.einsum('chk,chkd->cd', (g * act).astype(xc.dtype), expert_v[e])

    y = lax.map(chunk, xt).reshape(-1, D)[:n]
    return y.reshape(B, T, D)


def setup_inputs(seed: int = 0) -> dict:
    key = jax.random.key(seed)
    ks = jax.random.split(key, 24)
    f32 = jnp.float32
    n_pages = PAST_LEN // PAGE_SIZE
    n_pool = (5 * DEC_BATCH * n_pages) // 4

    def nrm(k, shape, scale=1.0):
        return jax.random.normal(k, shape, f32) * scale

    def gain(k, shape):
        return 1.0 + 0.05 * jax.random.normal(k, shape, f32)

    perm = jax.random.permutation(ks[6], n_pool)
    page_table = perm[:DEC_BATCH * n_pages].reshape(DEC_BATCH, n_pages).astype(jnp.int32)
    return {
        'x_prompt': nrm(ks[0], (BATCH, SEQ, D_MODEL)),
        'x_sample': nrm(ks[1], (DEC_BATCH, DEC_SEQ, D_MODEL)),
        'cache_mla_latent': nrm(ks[2], (DEPTH, n_pool, PAGE_SIZE, KV_LORA)),
        'cache_mla_rope': nrm(ks[3], (DEPTH, n_pool, PAGE_SIZE, MLA_ROPE)),
        'cache_moba_k': nrm(ks[4], (DEPTH, n_pool, PAGE_SIZE, MOBA_HEADS, MOBA_DH)),
        'cache_moba_v': nrm(ks[5], (DEPTH, n_pool, PAGE_SIZE, MOBA_HEADS, MOBA_DH)),
        'page_table': page_table,
        'attn_norm': gain(ks[7], (DEPTH, D_MODEL)),
        'w_in': nrm(ks[8], (DEPTH, D_MODEL, IN_WIDTH), D_MODEL ** -0.5),
        'q_a_norm': gain(ks[9], (DEPTH, Q_LORA)),
        'w_q_b': nrm(ks[10], (DEPTH, Q_LORA, MLA_HEADS * (MLA_NOPE + MLA_ROPE)), Q_LORA ** -0.5),
        'kv_a_norm': gain(ks[11], (DEPTH, KV_LORA)),
        'w_uk': nrm(ks[12], (DEPTH, KV_LORA, MLA_HEADS, MLA_NOPE), KV_LORA ** -0.5),
        'w_uv': nrm(ks[13], (DEPTH, KV_LORA, MLA_HEADS, MLA_V), KV_LORA ** -0.5),
        'w_up_a': nrm(ks[14], (DEPTH, MLA_HEADS * MLA_V, D_MODEL), (MLA_HEADS * MLA_V) ** -0.5),
        'w_up_b': nrm(ks[15], (DEPTH, MOBA_WIDTH, D_MODEL), MOBA_WIDTH ** -0.5),
        'w_o': nrm(ks[16], (DEPTH, D_MODEL, D_MODEL), D_MODEL ** -0.5),
        'ffn_norm': gain(ks[17], (DEPTH, D_MODEL)),
        'peer_w_query': nrm(ks[18], (DEPTH, D_MODEL, PEER_HEADS * PEER_DKEY), D_MODEL ** -0.5),
        'peer_keys_1': nrm(ks[19], (DEPTH, PEER_HEADS, PEER_NKEYS, PEER_DKEY // 2), (PEER_DKEY // 2) ** -0.5),
        'peer_keys_2': nrm(ks[20], (DEPTH, PEER_HEADS, PEER_NKEYS, PEER_DKEY // 2), (PEER_DKEY // 2) ** -0.5),
        'peer_u': nrm(ks[21], (DEPTH, PEER_EXPERTS, D_MODEL), D_MODEL ** -0.5),
        'peer_v': nrm(ks[22], (DEPTH, PEER_EXPERTS, D_MODEL), PEER_HEADS ** -0.5),
        'final_norm': gain(ks[23], (D_MODEL,)),
    }


def reference(x_prompt, x_sample, cache_mla_latent, cache_mla_rope, cache_moba_k, cache_moba_v, page_table,
              attn_norm, w_in, q_a_norm, w_q_b, kv_a_norm, w_uk, w_uv, w_up_a, w_up_b, w_o,
              ffn_norm, peer_w_query, peer_keys_1, peer_keys_2, peer_u, peer_v, final_norm):
    pos_p = jnp.arange(x_prompt.shape[1])
    pos_s = PAST_LEN + jnp.arange(x_sample.shape[1])
    page_means = jnp.mean(cache_moba_k, axis=2, dtype=jnp.float32)
    hp, hs = x_prompt, x_sample
    rows_p, rows_s = [], []
    for l in range(DEPTH):
        mix_w = (attn_norm[l], w_in[l], q_a_norm[l], w_q_b[l], kv_a_norm[l], w_uk[l])
        out_w = (w_uv[l], w_up_a[l], w_up_b[l], w_o[l])
        peer_w = (peer_w_query[l], peer_keys_1[l], peer_keys_2[l], peer_u[l], peer_v[l])
        q_lat, q_rope, c, kr, qm, km, vm, ga, gb = _mixer_inputs(hp, pos_p, *mix_w)
        hp = hp + _merge(_mla_prompt(q_lat, q_rope, c, kr), _moba_prompt(qm, km, vm), ga, gb, *out_w)
        hp = hp + _peer(_rmsnorm(hp, ffn_norm[l]), *peer_w)
        rows_p.append((c, kr, km, vm))
        q_lat, q_rope, c, kr, qm, km, vm, ga, gb = _mixer_inputs(hs, pos_s, *mix_w)
        o_a = _mla_sample(q_lat, q_rope, c, kr, cache_mla_latent, cache_mla_rope, page_table, l)
        o_b = _moba_sample(qm, km, vm, cache_moba_k, cache_moba_v, page_means[l], page_table, l)
        hs = hs + _merge(o_a, o_b, ga, gb, *out_w)
        hs = hs + _peer(_rmsnorm(hs, ffn_norm[l]), *peer_w)
        rows_s.append((c, kr, km, vm))
    y_prompt = _rmsnorm(hp, final_norm)
    y_sample = _rmsnorm(hs, final_norm)
    lat_p = jnp.stack([r[0] for r in rows_p])
    rope_p = jnp.stack([r[1] for r in rows_p])
    k_p = jnp.stack([r[2] for r in rows_p])
    v_p = jnp.stack([r[3] for r in rows_p])
    lat_s = jnp.stack([r[0] for r in rows_s])
    rope_s = jnp.stack([r[1] for r in rows_s])
    k_s = jnp.stack([r[2] for r in rows_s])
    v_s = jnp.stack([r[3] for r in rows_s])
    return (y_prompt, y_sample, lat_p, rope_p, k_p, v_p, lat_s, rope_s, k_s, v_s)
```

```python
import functools

import numpy as np
import jax
import jax.numpy as jnp
from jax import lax
from jax.experimental import pallas as pl
from jax.experimental.pallas import tpu as pltpu

F32 = jnp.float32
BF16 = jnp.bfloat16

D_MODEL = 1024
PAGE_SIZE = 128

MLA_HEADS = 8
MLA_NOPE = 64
MLA_ROPE = 32
MLA_V = 64
Q_LORA = 768
KV_LORA = 256
ROPE_THETA = 10000.0
MLA_SCALE = (MLA_NOPE + MLA_ROPE) ** -0.5
ROPE_PAD = 128
QCAT = KV_LORA + ROPE_PAD

MOBA_HEADS = 8
MOBA_DH = 64
MOBA_WIDTH = MOBA_HEADS * MOBA_DH
MOBA_BLOCK = 256
MOBA_TOPK = 3
MOBA_SCALE = MOBA_DH ** -0.5
MOBA_PAD = 128
MOBA_PADW = MOBA_HEADS * MOBA_PAD

PEER_HEADS = 8
PEER_NKEYS = 128
PEER_DKEY = 256
PEER_TOPK = 16
PEER_ECHUNK = 1024
PEER_ICHUNK = PEER_ECHUNK // PEER_NKEYS

RMS_EPS = 1e-6
NEG_INF = -1e30

VMEM_LIMIT_BYTES = 56 * 1024 * 1024


def _cparams(*sem):
    return pltpu.CompilerParams(dimension_semantics=sem, vmem_limit_bytes=VMEM_LIMIT_BYTES)


def _rms(xf, w):
    return xf * lax.rsqrt(jnp.mean(xf * xf, axis=-1, keepdims=True) + RMS_EPS) * w


def _dot_nt(a, b):
    return lax.dot_general(a, b, (((1,), (1,)), ((), ())), preferred_element_type=F32)


def _full_spec(shape):
    n = len(shape)
    return pl.BlockSpec(shape, lambda *_: (0,) * n)


_A_QA = 0
_A_CKV = Q_LORA
_A_QM = _A_CKV + KV_LORA
_A_KM = _A_QM + MOBA_WIDTH
_A_VM = _A_KM + MOBA_WIDTH
_A_KR = _A_VM + MOBA_WIDTH
_A_KRS = _A_KR + ROPE_PAD
_A_WIDTH = _A_KRS + ROPE_PAD
_Q_NOPE = 0
_Q_ROPE = MLA_HEADS * MLA_NOPE
_Q_ROPES = _Q_ROPE + MLA_HEADS * ROPE_PAD
_Q_WIDTH = _Q_ROPES + MLA_HEADS * ROPE_PAD


def _pad_heads(v, lane_lo):
    out = []
    for p in range(MOBA_HEADS // 2):
        blk = v[:, p * 128:(p + 1) * 128]
        out.append(jnp.where(lane_lo, blk, 0.0))
        out.append(jnp.where(lane_lo, pltpu.roll(blk, MOBA_DH, axis=1), 0.0))
    return jnp.concatenate(out, axis=1)


def _inproj_body(x_ref, cs_ref, sn_ref, an_ref, wa_ref, qan_ref, wqb_ref, kvn_ref, wuk_ref,
                 qcat_ref, kcat_ref, c_ref, kr_ref, qm_ref, km_ref, vm_ref, qp_ref, kp_ref, vp_ref):
    xn = _rms(x_ref[...], an_ref[...]).astype(BF16)
    z = jnp.dot(xn, wa_ref[...], preferred_element_type=F32)
    cs = cs_ref[...]
    sn = sn_ref[...]
    qn = _rms(z[:, _A_QA:_A_QA + Q_LORA], qan_ref[...]).astype(BF16)
    q = jnp.dot(qn, wqb_ref[...], preferred_element_type=F32)
    for p in range(MLA_HEADS // 2):
        lat = jnp.dot(q[:, 128 * p:128 * (p + 1)].astype(BF16), wuk_ref[p], preferred_element_type=F32)
        for hl in range(2):
            h = 2 * p + hl
            qcat_ref[:, h * QCAT:h * QCAT + KV_LORA] = (lat[:, hl * KV_LORA:(hl + 1) * KV_LORA] * MLA_SCALE).astype(BF16)
            rope = (q[:, _Q_ROPE + h * ROPE_PAD:_Q_ROPE + (h + 1) * ROPE_PAD] * cs
                    + q[:, _Q_ROPES + h * ROPE_PAD:_Q_ROPES + (h + 1) * ROPE_PAD] * sn)
            qcat_ref[:, h * QCAT + KV_LORA:(h + 1) * QCAT] = (rope * MLA_SCALE).astype(BF16)
    c = _rms(z[:, _A_CKV:_A_CKV + KV_LORA], kvn_ref[...])
    c_ref[...] = c
    kr = z[:, _A_KR:_A_KR + ROPE_PAD] * cs + z[:, _A_KRS:_A_KRS + ROPE_PAD] * sn
    kr_ref[...] = kr[:, :MLA_ROPE]
    kcat_ref[:, :KV_LORA] = c.astype(BF16)
    kcat_ref[:, KV_LORA:] = kr.astype(BF16)
    qm = z[:, _A_QM:_A_QM + MOBA_WIDTH]
    km = z[:, _A_KM:_A_KM + MOBA_WIDTH]
    vm = z[:, _A_VM:_A_VM + MOBA_WIDTH]
    qm_ref[...] = qm
    km_ref[...] = km
    vm_ref[...] = vm
    lane_lo = lax.broadcasted_iota(jnp.int32, (qm.shape[0], 128), 1) < MOBA_DH
    qp_ref[...] = _pad_heads(qm * MOBA_SCALE, lane_lo).astype(BF16)
    kp_ref[...] = _pad_heads(km, lane_lo).astype(BF16)
    vp_ref[...] = _pad_heads(vm, lane_lo).astype(BF16)


def _inproj(x2, cs, sn, w, tm):
    n = x2.shape[0]
    tab_blocks = cs.shape[0] // tm
    row = lambda i: (i, 0)
    tab = lambda i: (i % tab_blocks, 0)
    widths = [(MLA_HEADS * QCAT, BF16), (QCAT, BF16), (KV_LORA, F32), (MLA_ROPE, F32),
              (MOBA_WIDTH, F32), (MOBA_WIDTH, F32), (MOBA_WIDTH, F32),
              (MOBA_PADW, BF16), (MOBA_PADW, BF16), (MOBA_PADW, BF16)]
    return pl.pallas_call(
        _inproj_body,
        grid=(n // tm,),
        in_specs=[pl.BlockSpec((tm, D_MODEL), row),
                  pl.BlockSpec((tm, ROPE_PAD), tab), pl.BlockSpec((tm, ROPE_PAD), tab),
                  _full_spec((1, D_MODEL)), _full_spec((D_MODEL, _A_WIDTH)),
                  _full_spec((1, Q_LORA)), _full_spec((Q_LORA, _Q_WIDTH)),
                  _full_spec((1, KV_LORA)), _full_spec((MLA_HEADS // 2, 128, 2 * KV_LORA))],
        out_specs=[pl.BlockSpec((tm, wd), row) for wd, _ in widths],
        out_shape=[jax.ShapeDtypeStruct((n, wd), dt) for wd, dt in widths],
        compiler_params=_cparams("arbitrary"),
        name="inproj",
    )(x2, cs, sn, w["attn_norm"], w["w_a"], w["q_a_norm"], w["w_qb"], w["kv_a_norm"], w["w_uk_pairs"])


def _softmax_step(s, v, m_scr, l_scr, acc_scr):
    m_prev = m_scr[...]
    m_new = jnp.maximum(m_prev, jnp.max(s, axis=-1, keepdims=True))
    alpha = jnp.exp(m_prev - m_new)
    p = jnp.exp(s - m_new)
    l_scr[...] = alpha * l_scr[...] + jnp.sum(p, axis=-1, keepdims=True)
    acc_scr[...] = alpha * acc_scr[...] + jnp.dot(p.astype(BF16), v, preferred_element_type=F32)
    m_scr[...] = m_new


def _mla_prompt_body(qcat_ref, kcat_ref, wuv_ref, out_ref, q_scr, m_scr, l_scr, acc_scr, *, tq):
    qi = pl.program_id(1)
    for h in range(MLA_HEADS):
        q_scr[h * tq:(h + 1) * tq, :] = qcat_ref[:, h * QCAT:(h + 1) * QCAT]
    m_scr[...] = jnp.full(m_scr.shape, NEG_INF, F32)
    l_scr[...] = jnp.zeros(l_scr.shape, F32)
    acc_scr[...] = jnp.zeros(acc_scr.shape, F32)

    def chunk(kc, masked):
        k = kcat_ref[pl.ds(pl.multiple_of(kc * tq, tq), tq), :]
        s = _dot_nt(q_scr[...], k)
        if masked:
            tok = lax.broadcasted_iota(jnp.int32, s.shape, 0) & (tq - 1)
            col = lax.broadcasted_iota(jnp.int32, s.shape, 1)
            s = jnp.where(col <= tok, s, NEG_INF)
        _softmax_step(s, k[:, :KV_LORA], m_scr, l_scr, acc_scr)

    def past(kc, carry):
        chunk(kc, False)
        return carry

    lax.fori_loop(0, qi, past, 0)
    chunk(qi, True)
    o = (acc_scr[...] / l_scr[...]).astype(BF16)
    y = jnp.zeros(out_ref.shape, F32)
    for h in range(MLA_HEADS):
        y = y + jnp.dot(o[h * tq:(h + 1) * tq], wuv_ref[h], preferred_element_type=F32)
    out_ref[...] = y.astype(out_ref.dtype)


def _mla_prompt(qcat, kcat, wuv_pad, batch, seq, tq):
    nq = seq // tq
    rows = MLA_HEADS * tq
    return pl.pallas_call(
        functools.partial(_mla_prompt_body, tq=tq),
        grid=(batch, nq),
        in_specs=[pl.BlockSpec((tq, MLA_HEADS * QCAT), lambda b, i: (b * nq + i, 0)),
                  pl.BlockSpec((seq, QCAT), lambda b, i: (b, 0)),
                  _full_spec((MLA_HEADS, KV_LORA, MLA_HEADS * MLA_V))],
        out_specs=pl.BlockSpec((tq, MLA_HEADS * MLA_V), lambda b, i: (b * nq + i, 0)),
        out_shape=jax.ShapeDtypeStruct((batch * seq, MLA_HEADS * MLA_V), BF16),
        scratch_shapes=[pltpu.VMEM((rows, QCAT), BF16), pltpu.VMEM((rows, 1), F32),
                        pltpu.VMEM((rows, 1), F32), pltpu.VMEM((rows, KV_LORA), F32)],
        compiler_params=_cparams("arbitrary", "arbitrary"),
        name="mla_prompt",
    )(qcat, kcat, wuv_pad)


def _alibi_slope(h):
    return 2.0 ** (-8.0 * (h + 1) / MOBA_HEADS)


def _block_choice(scores, n_valid):
    lane = lax.broadcasted_iota(jnp.int32, scores.shape, 1)
    n = lane & 7
    s = jnp.where(n < n_valid, scores, NEG_INF)
    beaten = jnp.zeros(scores.shape, F32)
    for d in range(1, 8):
        other = jnp.where(n >= d, pltpu.roll(s, d, axis=1), pltpu.roll(s, 128 - 8 + d, axis=1))
        wins = (other > s) | ((other == s) & (n >= d))
        beaten = beaten + wins.astype(F32)
    return jnp.where((beaten < MOBA_TOPK) & (n < n_valid), 1.0, 0.0)


def _moba_prompt_body(qf_ref, kf_ref, qp_ref, kp_ref, vp_ref, out_ref, kmean_scr, sel_scr, m_scr, l_scr, acc_scr):
    qi = pl.program_id(1)
    j = pl.program_id(2)
    nblk = kmean_scr.shape[0]
    blk = MOBA_BLOCK

    @pl.when((qi == 0) & (j == 0))
    def _():
        kmean_scr[...] = jnp.zeros(kmean_scr.shape, F32)

    @pl.when(j == 0)
    def _():
        m_scr[...] = jnp.full(m_scr.shape, NEG_INF, F32)
        l_scr[...] = jnp.zeros(l_scr.shape, F32)
        acc_scr[...] = jnp.zeros(acc_scr.shape, F32)
        km = kmean_scr[...]
        tiled = jnp.concatenate([km] * MOBA_HEADS + [jnp.zeros((128 - MOBA_HEADS * nblk, MOBA_WIDTH), F32)], axis=0)
        row_h = lax.broadcasted_iota(jnp.int32, tiled.shape, 0) // nblk
        lane_h = lax.broadcasted_iota(jnp.int32, tiled.shape, 1) // MOBA_DH
        kmbd = jnp.where(row_h == lane_h, tiled, 0.0)
        scores = lax.dot_general(qf_ref[...], kmbd, (((1,), (1,)), ((), ())),
                                 preferred_element_type=F32, precision=lax.Precision.HIGHEST)
        sel_scr[...] = _block_choice(scores, qi)

    @pl.when(j <= qi)
    def _():
        diag = (j == qi).astype(jnp.int32)
        r = lax.broadcasted_iota(jnp.int32, (blk, blk), 0)
        c = lax.broadcasted_iota(jnp.int32, (blk, blk), 1)
        dist = ((qi - j) * blk + (r - c)).astype(F32)
        causal = (r >= c).astype(jnp.int32) * diag
        lane = lax.broadcasted_iota(jnp.int32, sel_scr.shape, 1)
        sel = sel_scr[...]
        for h in range(MOBA_HEADS):
            sl = slice(h * MOBA_PAD, (h + 1) * MOBA_PAD)
            s = _dot_nt(qp_ref[:, sl], kp_ref[:, sl])
            s = s - _alibi_slope(h) * dist
            chosen = jnp.sum(jnp.where(lane == h * 8 + j, sel, 0.0), axis=-1, keepdims=True)
            allowed = causal + (chosen > 0.5).astype(jnp.int32) * (1 - diag)
            s = jnp.where(allowed > 0, s, NEG_INF)
            _softmax_step(s, vp_ref[:, sl], m_scr.at[h], l_scr.at[h], acc_scr.at[h])

    @pl.when(j == qi)
    def _():
        kmean_scr[pl.ds(qi, 1), :] = jnp.mean(kf_ref[...], axis=0, keepdims=True)
        for h in range(MOBA_HEADS):
            out_ref[:, h * MOBA_PAD:(h + 1) * MOBA_PAD] = (acc_scr[h] / l_scr[h]).astype(out_ref.dtype)


def _moba_prompt(qm, km, qp, kp, vp, batch, seq):
    nb = seq // MOBA_BLOCK
    assert nb <= 8 and seq % MOBA_BLOCK == 0
    qmap = lambda b, i, j: (b * nb + i, 0)
    kmap = lambda b, i, j: (b * nb + jnp.minimum(i, j), 0)
    blk = MOBA_BLOCK
    return pl.pallas_call(
        _moba_prompt_body,
        grid=(batch, nb, nb),
        in_specs=[pl.BlockSpec((blk, MOBA_WIDTH), qmap), pl.BlockSpec((blk, MOBA_WIDTH), qmap),
                  pl.BlockSpec((blk, MOBA_PADW), qmap),
                  pl.BlockSpec((blk, MOBA_PADW), kmap), pl.BlockSpec((blk, MOBA_PADW), kmap)],
        out_specs=pl.BlockSpec((blk, MOBA_PADW), qmap),
        out_shape=jax.ShapeDtypeStruct((batch * seq, MOBA_PADW), BF16),
        scratch_shapes=[pltpu.VMEM((8, MOBA_WIDTH), F32), pltpu.VMEM((blk, 128), F32),
                        pltpu.VMEM((MOBA_HEADS, blk, 1), F32), pltpu.VMEM((MOBA_HEADS, blk, 1), F32),
                        pltpu.VMEM((MOBA_HEADS, blk, MOBA_PAD), F32)],
        compiler_params=_cparams("arbitrary", "arbitrary", "arbitrary"),
        name="moba_prompt",
    )(qm, km, qp, kp, vp)


def _sigmoid(g):
    return 1.0 / (1.0 + jnp.exp(-g))


def _merge_body(x_ref, ya_ref, ob_ref, an_ref, wg_ref, wua_ref, wub_ref, wo_ref, h_ref):
    x = x_ref[...]
    xn = _rms(x, an_ref[...]).astype(BF16)
    g = jnp.dot(xn, wg_ref[...], preferred_element_type=F32)
    y_a = jnp.dot(ya_ref[...], wua_ref[...], preferred_element_type=F32)
    y_b = jnp.dot(ob_ref[...], wub_ref[...], preferred_element_type=F32)
    mix = _sigmoid(g[:, :D_MODEL]) * y_a + _sigmoid(g[:, D_MODEL:]) * y_b
    h_ref[...] = x + jnp.dot(mix.astype(BF16), wo_ref[...], preferred_element_type=F32)


def _merge(x2, ya, ob, w_up_b, w, tm):
    n = x2.shape[0]
    wb = ob.shape[1]
    row = lambda i: (i, 0)
    return pl.pallas_call(
        _merge_body,
        grid=(n // tm,),
        in_specs=[pl.BlockSpec((tm, D_MODEL), row), pl.BlockSpec((tm, MLA_HEADS * MLA_V), row),
                  pl.BlockSpec((tm, wb), row),
                  _full_spec((1, D_MODEL)), _full_spec((D_MODEL, 2 * D_MODEL)),
                  _full_spec((MLA_HEADS * MLA_V, D_MODEL)), _full_spec((wb, D_MODEL)),
                  _full_spec((D_MODEL, D_MODEL))],
        out_specs=pl.BlockSpec((tm, D_MODEL), row),
        out_shape=jax.ShapeDtypeStruct((n, D_MODEL), F32),
        compiler_params=_cparams("arbitrary"),
        name="merge",
    )(x2, ya, ob, w["attn_norm"], w["w_g"], w["w_up_a"], w_up_b, w["w_o"])


def _topk_ranks(s, key):
    rank = jnp.full(s.shape, float(PEER_TOPK), F32)
    work = s
    vals = []
    for r in range(PEER_TOPK):
        m = jnp.max(work, axis=0, keepdims=True)
        kmin = jnp.min(jnp.where(work == m, key, 1e9), axis=0, keepdims=True)
        hit = key == kmin
        rank = jnp.where(hit, float(r), rank)
        work = jnp.where(hit, -jnp.inf, work)
        vals.append(m)
    return rank, jnp.concatenate(vals, axis=0)


def _gelu(x):
    return 0.5 * x * (1.0 + lax.erf(x * np.float32(np.sqrt(0.5))))


def _peer_body(h_ref, fn_ref, wq_ref, k1_ref, k2_ref, u_ref, vt_ref, on_ref, y_ref,
               xn_scr, qt_scr, r2_scr, e2_scr, nb_scr, p1_scr, wg_scr, acc_scr):
    c = pl.program_id(1)
    t = h_ref.shape[0]
    kk = PEER_TOPK

    @pl.when(c == 0)
    def _():
        xn = _rms(h_ref[...], fn_ref[...]).astype(BF16)
        xn_scr[...] = xn
        qt_scr[...] = _dot_nt(wq_ref[...], xn)
        acc_scr[...] = jnp.zeros(acc_scr.shape, F32)
        key128 = lax.broadcasted_iota(jnp.int32, (PEER_NKEYS, t), 0).astype(F32)
        crow = lax.broadcasted_iota(jnp.int32, (kk * kk, t), 0)
        ckey = ((crow % kk) * kk + crow // kk).astype(F32)

        def per_head(hd, carry):
            base = pl.multiple_of(hd * PEER_DKEY, PEER_DKEY)
            q1 = qt_scr[pl.ds(base, PEER_DKEY // 2), :].astype(BF16)
            q2 = qt_scr[pl.ds(base + PEER_DKEY // 2, PEER_DKEY // 2), :].astype(BF16)
            s1 = jnp.dot(k1_ref[hd], q1, preferred_element_type=F32)
            s2 = jnp.dot(k2_ref[hd], q2, preferred_element_type=F32)
            rank1, v1 = _topk_ranks(s1, key128)
            rank2, v2 = _topk_ranks(s2, key128)
            cand = (jnp.concatenate([v1] * kk, axis=0)
                    + jnp.concatenate([jnp.broadcast_to(v2[b:b + 1], (kk, t)) for b in range(kk)], axis=0))
            rankc, _ = _topk_ranks(cand, ckey)
            selc = jnp.where(rankc < float(kk), 1.0, 0.0)
            e1 = jnp.exp(v1 - v1[0:1])
            e2 = jnp.exp(v2 - v2[0:1])
            nb = jnp.zeros((kk, t), F32)
            zsum = jnp.zeros((1, t), F32)
            for b in range(kk):
                sel_b = selc[b * kk:(b + 1) * kk]
                nb = nb + sel_b
                zsum = zsum + e2[b:b + 1] * jnp.sum(sel_b * e1, axis=0, keepdims=True)
            e2 = e2 / zsum
            nbx = jnp.zeros((PEER_NKEYS, t), F32)
            p1x = jnp.zeros((PEER_NKEYS, t), F32)
            e2x = jnp.zeros((PEER_NKEYS, t), F32)
            for a in range(kk):
                hit1 = rank1 == float(a)
                nbx = jnp.where(hit1, nb[a:a + 1], nbx)
                p1x = jnp.where(hit1, e1[a:a + 1], p1x)
                e2x = jnp.where(rank2 == float(a), e2[a:a + 1], e2x)
            r2_scr[hd] = rank2
            e2_scr[hd] = e2x
            nb_scr[hd] = nbx
            p1_scr[hd] = p1x
            return carry

        lax.fori_loop(0, PEER_HEADS, per_head, 0)

    act = _dot_nt(u_ref[...], xn_scr[...])
    for ii in range(PEER_ICHUNK):
        i = c * PEER_ICHUNK + ii
        wgt = jnp.zeros((PEER_NKEYS, t), F32)
        for hd in range(PEER_HEADS):
            nbrow = nb_scr[hd, pl.ds(i, 1), :]
            p1row = p1_scr[hd, pl.ds(i, 1), :]
            wgt = wgt + jnp.where(r2_scr[hd] < nbrow, e2_scr[hd] * p1row, 0.0)
        rows = slice(ii * PEER_NKEYS, (ii + 1) * PEER_NKEYS)
        wg_scr[rows, :] = (wgt * _gelu(act[rows])).astype(BF16)
    acc_scr[...] += jnp.dot(vt_ref[...], wg_scr[...], preferred_element_type=F32)

    @pl.when(c == pl.num_programs(1) - 1)
    def _():
        h2 = h_ref[...] + acc_scr[...].T
        y_ref[...] = _rms(h2, on_ref[...])


def _peer(h2d, w, t):
    n = h2d.shape[0]
    n_exp = w["peer_u"].shape[0]
    nchunk = n_exp // PEER_ECHUNK
    slab = (PEER_HEADS, PEER_NKEYS, t)
    return pl.pallas_call(
        _peer_body,
        grid=(n // t, nchunk),
        in_specs=[pl.BlockSpec((t, D_MODEL), lambda i, c: (i, 0)),
                  _full_spec((1, D_MODEL)), _full_spec((PEER_HEADS * PEER_DKEY, D_MODEL)),
                  _full_spec((PEER_HEADS, PEER_NKEYS, PEER_DKEY // 2)),
                  _full_spec((PEER_HEADS, PEER_NKEYS, PEER_DKEY // 2)),
                  pl.BlockSpec((PEER_ECHUNK, D_MODEL), lambda i, c: (c, 0)),
                  pl.BlockSpec((D_MODEL, PEER_ECHUNK), lambda i, c: (0, c)),
                  _full_spec((1, D_MODEL))],
        out_specs=pl.BlockSpec((t, D_MODEL), lambda i, c: (i, 0)),
        out_shape=jax.ShapeDtypeStruct((n, D_MODEL), F32),
        scratch_shapes=[pltpu.VMEM((t, D_MODEL), BF16), pltpu.VMEM((PEER_HEADS * PEER_DKEY, t), F32),
                        pltpu.VMEM(slab, F32), pltpu.VMEM(slab, F32), pltpu.VMEM(slab, F32), pltpu.VMEM(slab, F32),
                        pltpu.VMEM((PEER_ECHUNK, t), BF16), pltpu.VMEM((D_MODEL, t), F32)],
        compiler_params=_cparams("arbitrary", "arbitrary"),
        name="peer",
    )(h2d, w["ffn_norm"], w["peer_wq_t"], w["peer_k1"], w["peer_k2"], w["peer_u"], w["peer_vt"], w["final_norm"])


SAMPLE_PAGES_PER_STEP = 8


def _mla_sample_body(pt_ref, q_ref, knew_ref, *rest, pages, dec):
    c_refs = rest[:pages]
    r_refs = rest[pages:2 * pages]
    wuv_ref, out_ref, q_scr, knew_scr, m_scr, l_scr, acc_scr = rest[2 * pages:]
    s_id = pl.program_id(1)
    rows = MLA_HEADS * dec

    @pl.when(s_id == 0)
    def _():
        for h in range(MLA_HEADS):
            q_scr[h * dec:(h + 1) * dec, :] = q_ref[0, :, h * QCAT:(h + 1) * QCAT].astype(F32)
        m_scr[...] = jnp.full(m_scr.shape, NEG_INF, F32)
        l_scr[...] = jnp.zeros(l_scr.shape, F32)
        acc_scr[...] = jnp.zeros(acc_scr.shape, F32)

    q = q_scr[...].astype(BF16)
    q_lat = q[:, :KV_LORA]
    q_rope = q[:, KV_LORA:KV_LORA + MLA_ROPE]
    cs = [c_refs[i][0].astype(BF16) for i in range(pages)]
    s = jnp.concatenate([_dot_nt(q_lat, cs[i]) + _dot_nt(q_rope, r_refs[i][0].astype(BF16))
                         for i in range(pages)], axis=1)
    m_prev = m_scr[...]
    m_new = jnp.maximum(m_prev, jnp.max(s, axis=-1, keepdims=True))
    alpha = jnp.exp(m_prev - m_new)
    p = jnp.exp(s - m_new)
    l_scr[...] = alpha * l_scr[...] + jnp.sum(p, axis=-1, keepdims=True)
    p = p.astype(BF16)
    pv = jnp.zeros(acc_scr.shape, F32)
    for i in range(pages):
        pv = pv + jnp.dot(p[:, i * PAGE_SIZE:(i + 1) * PAGE_SIZE], cs[i], preferred_element_type=F32)
    acc_scr[...] = alpha * acc_scr[...] + pv
    m_scr[...] = m_new

    @pl.when(s_id == pl.num_programs(1) - 1)
    def _():
        knew_scr[...] = jnp.zeros(knew_scr.shape, F32)
        knew_scr[0:dec, :] = knew_ref[0].astype(F32)
        k = knew_scr[...].astype(BF16)
        s_new = _dot_nt(q, k)
        tok = lax.broadcasted_iota(jnp.int32, s_new.shape, 0) & (dec - 1)
        col = lax.broadcasted_iota(jnp.int32, s_new.shape, 1)
        s_new = jnp.where(col <= tok, s_new, NEG_INF)
        _softmax_step(s_new, k[:, :KV_LORA], m_scr, l_scr, acc_scr)
        o = (acc_scr[...] / l_scr[...]).astype(BF16)
        y = jnp.zeros((dec, MLA_HEADS * MLA_V), F32)
        for h in range(MLA_HEADS):
            y = y + jnp.dot(o[h * dec:(h + 1) * dec], wuv_ref[h], preferred_element_type=F32)
        out_ref[0] = y.astype(out_ref.dtype)


def _mla_sample(page_flat, qcat3, kcat3, cache_c, cache_r, wuv_pad, n_pages):
    bd, dec, _ = qcat3.shape
    assert dec & (dec - 1) == 0 and dec <= PAGE_SIZE
    pages = min(SAMPLE_PAGES_PER_STEP, n_pages)
    steps = n_pages // pages
    rows = MLA_HEADS * dec

    def page_map(i):
        return lambda b, s, pt: (pt[b * n_pages + s * pages + i], 0, 0)

    per_b = lambda b, s, pt: (b, 0, 0)
    grid_spec = pltpu.PrefetchScalarGridSpec(
        num_scalar_prefetch=1,
        grid=(bd, steps),
        in_specs=([pl.BlockSpec((1, dec, MLA_HEADS * QCAT), per_b), pl.BlockSpec((1, dec, QCAT), per_b)]
                  + [pl.BlockSpec((1, PAGE_SIZE, KV_LORA), page_map(i)) for i in range(pages)]
                  + [pl.BlockSpec((1, PAGE_SIZE, MLA_ROPE), page_map(i)) for i in range(pages)]
                  + [pl.BlockSpec((MLA_HEADS, KV_LORA, MLA_HEADS * MLA_V), lambda b, s, pt: (0, 0, 0))]),
        out_specs=pl.BlockSpec((1, dec, MLA_HEADS * MLA_V), per_b),
        scratch_shapes=[pltpu.VMEM((rows, QCAT), F32), pltpu.VMEM((PAGE_SIZE, QCAT), F32),
                        pltpu.VMEM((rows, 1), F32), pltpu.VMEM((rows, 1), F32), pltpu.VMEM((rows, KV_LORA), F32)],
    )
    return pl.pallas_call(
        functools.partial(_mla_sample_body, pages=pages, dec=dec),
        grid_spec=grid_spec,
        out_shape=jax.ShapeDtypeStruct((bd, dec, MLA_HEADS * MLA_V), BF16),
        compiler_params=_cparams("arbitrary", "arbitrary"),
        name="mla_sample",
    )(page_flat, qcat3, kcat3, *([cache_c] * pages), *([cache_r] * pages), wuv_pad)


def _group_choice(scores, group, n_valid, topk):
    lane = lax.broadcasted_iota(jnp.int32, scores.shape, 1)
    s = jnp.where(lane < n_valid, scores, NEG_INF)
    beaten = jnp.zeros(scores.shape, F32)
    for d in range(1, group):
        other = jnp.where(lane >= d, pltpu.roll(s, d, axis=1), pltpu.roll(s, 128 - group + d, axis=1))
        wins = (other > s) | ((other == s) & (lane >= d))
        beaten = beaten + wins.astype(F32)
    return jnp.where((beaten < topk) & (lane < n_valid), 1.0, 0.0)


def _moba_sample_body(pt_ref, qm_ref, kn_ref, vn_ref, *rest, pages, dec, n_pages):
    k_refs = rest[:pages]
    v_refs = rest[pages:2 * pages]
    out_ref, qbd_scr, kmean_scr, s_scr, snew_scr, sel_scr, new_scr, l_scr, acc_scr = rest[2 * pages:]
    s_id = pl.program_id(1)
    n_k = n_pages // pages
    rows = MOBA_HEADS * dec
    n_blocks = n_pages // 2
    past = n_pages * PAGE_SIZE
    row = lax.broadcasted_iota(jnp.int32, (rows, PAGE_SIZE), 0)
    col = lax.broadcasted_iota(jnp.int32, (rows, PAGE_SIZE), 1)
    tok = row & (dec - 1)
    slope = jnp.zeros((rows, PAGE_SIZE), F32)
    for h in range(MOBA_HEADS):
        slope = jnp.where(row // dec == h, _alibi_slope(h), slope)

    @pl.when(s_id == 0)
    def _():
        tiled = jnp.concatenate([qm_ref[0]] * MOBA_HEADS, axis=0)
        row_h = lax.broadcasted_iota(jnp.int32, tiled.shape, 0) // dec
        lane_h = lax.broadcasted_iota(jnp.int32, tiled.shape, 1) // MOBA_DH
        qbd_scr[...] = jnp.where(row_h == lane_h, tiled * MOBA_SCALE, 0.0)
        kmean_scr[...] = jnp.zeros(kmean_scr.shape, F32)

    @pl.when(s_id < n_k)
    def _():
        qbd = qbd_scr[...].astype(BF16)
        for i in range(0, pages, 2):
            k0 = k_refs[i][0]
            k1 = k_refs[i + 1][0]
            page = s_id * pages + i
            s_scr[page] = _dot_nt(qbd, k0.astype(BF16))
            s_scr[page + 1] = _dot_nt(qbd, k1.astype(BF16))
            kmean_scr[pl.ds(page // 2, 1), :] = 0.5 * (jnp.mean(k0, axis=0, keepdims=True)
                                                         + jnp.mean(k1, axis=0, keepdims=True))

    @pl.when(s_id == n_k)
    def _():
        scores = lax.dot_general(qbd_scr[...], kmean_scr[...], (((1,), (1,)), ((), ())),
                                 preferred_element_type=F32, precision=lax.Precision.HIGHEST)
        group = max(n_blocks, 1)
        sel = _group_choice(scores, group, n_blocks, MOBA_TOPK)
        sel_scr[...] = sel
        new_scr[...] = jnp.zeros(new_scr.shape, F32)
        new_scr[0:dec, :] = kn_ref[0]
        s_new = _dot_nt(qbd_scr[...].astype(BF16), new_scr[...].astype(BF16))
        s_new = s_new - slope * (tok - col).astype(F32)
        s_new = jnp.where(col <= tok, s_new, NEG_INF)
        lane = lax.broadcasted_iota(jnp.int32, sel.shape, 1)

        def masked_scores(page):
            chosen = jnp.sum(jnp.where(lane == page // 2, sel, 0.0), axis=-1, keepdims=True)
            dist = (past + tok - page * PAGE_SIZE - col).astype(F32)
            return jnp.where(chosen > 0.5, s_scr[page] - slope * dist, NEG_INF)

        def max_body(page, m):
            return jnp.maximum(m, jnp.max(masked_scores(page), axis=-1, keepdims=True))

        m = lax.fori_loop(0, n_pages, max_body, jnp.max(s_new, axis=-1, keepdims=True))

        def exp_body(page, l):
            p = jnp.exp(masked_scores(page) - m)
            s_scr[page] = p
            return l + jnp.sum(p, axis=-1, keepdims=True)

        p_new = jnp.exp(s_new - m)
        snew_scr[...] = p_new
        l_scr[...] = lax.fori_loop(0, n_pages, exp_body, jnp.sum(p_new, axis=-1, keepdims=True))
        acc_scr[...] = jnp.zeros(acc_scr.shape, F32)

    @pl.when(s_id >= n_k)
    def _():
        pv = jnp.zeros(acc_scr.shape, F32)
        for i in range(pages):
            page = (s_id - n_k) * pages + i
            pv = pv + jnp.dot(s_scr[page].astype(BF16), v_refs[i][0].astype(BF16), preferred_element_type=F32)
        acc_scr[...] += pv

    @pl.when(s_id == 2 * n_k - 1)
    def _():
        new_scr[...] = jnp.zeros(new_scr.shape, F32)
        new_scr[0:dec, :] = vn_ref[0]
        acc = acc_scr[...] + jnp.dot(snew_scr[...].astype(BF16), new_scr[...].astype(BF16),
                                     preferred_element_type=F32)
        o = acc / l_scr[...]
        row_h = lax.broadcasted_iota(jnp.int32, o.shape, 0) // dec
        lane_h = lax.broadcasted_iota(jnp.int32, o.shape, 1) // MOBA_DH
        o = jnp.where(row_h == lane_h, o, 0.0)
        y = o[0:dec]
        for h in range(1, MOBA_HEADS):
            y = y + o[h * dec:(h + 1) * dec]
        out_ref[0] = y.astype(out_ref.dtype)


def _moba_sample(page_flat, qm3, km3, vm3, cache_k, cache_v, n_pages):
    bd, dec, _ = qm3.shape
    assert dec == 8, "row groups of the stacked heads must be one sublane tile"
    assert n_pages % 2 == 0, "the past must be a whole number of MoBA blocks"
    n_blocks = n_pages // 2
    assert n_blocks & (n_blocks - 1) == 0 and n_blocks <= 128
    pages = min(SAMPLE_PAGES_PER_STEP, n_pages)
    n_k = n_pages // pages
    rows = MOBA_HEADS * dec

    def k_map(i):
        return lambda b, s, pt: (pt[b * n_pages + jnp.minimum(s, n_k - 1) * pages + i], 0, 0)

    def v_map(i):
        return lambda b, s, pt: (pt[b * n_pages + jnp.maximum(s - n_k, 0) * pages + i], 0, 0)

    per_b = lambda b, s, pt: (b, 0, 0)
    grid_spec = pltpu.PrefetchScalarGridSpec(
        num_scalar_prefetch=1,
        grid=(bd, 2 * n_k),
        in_specs=([pl.BlockSpec((1, dec, MOBA_WIDTH), per_b)] * 3
                  + [pl.BlockSpec((1, PAGE_SIZE, MOBA_WIDTH), k_map(i)) for i in range(pages)]
                  + [pl.BlockSpec((1, PAGE_SIZE, MOBA_WIDTH), v_map(i)) for i in range(pages)]),
        out_specs=pl.BlockSpec((1, dec, MOBA_WIDTH), per_b),
        scratch_shapes=[pltpu.VMEM((rows, MOBA_WIDTH), F32), pltpu.VMEM((128, MOBA_WIDTH), F32),
                        pltpu.VMEM((n_pages, rows, PAGE_SIZE), F32), pltpu.VMEM((rows, PAGE_SIZE), F32),
                        pltpu.VMEM((rows, 128), F32), pltpu.VMEM((PAGE_SIZE, MOBA_WIDTH), F32),
                        pltpu.VMEM((rows, 1), F32), pltpu.VMEM((rows, MOBA_WIDTH), F32)],
    )
    return pl.pallas_call(
        functools.partial(_moba_sample_body, pages=pages, dec=dec, n_pages=n_pages),
        grid_spec=grid_spec,
        out_shape=jax.ShapeDtypeStruct((bd, dec, MOBA_WIDTH), BF16),
        compiler_params=_cparams("arbitrary", "arbitrary"),
        name="moba_sample",
    )(page_flat, qm3, km3, vm3, *([cache_k] * pages), *([cache_v] * pages))


def _rope_tables(pos):
    half = MLA_ROPE // 2
    inv = ROPE_THETA ** (-jnp.arange(half, dtype=F32) / half)
    ang = pos.astype(F32)[:, None] * inv
    cos, sin = jnp.cos(ang), jnp.sin(ang)
    pad = jnp.zeros((pos.shape[0], ROPE_PAD - MLA_ROPE), F32)
    return jnp.concatenate([cos, cos, pad], axis=1), jnp.concatenate([-sin, sin, pad], axis=1)


def _swap_halves(w):
    half = MLA_ROPE // 2
    return jnp.concatenate([w[..., half:], w[..., :half]], axis=-1)


def _layer_weights(l, attn_norm, w_in, q_a_norm, w_q_b, kv_a_norm, w_uk, w_uv, w_up_a, w_up_b, w_o,
                   ffn_norm, peer_w_query, peer_keys_1, peer_keys_2, peer_u, peer_v, final_norm):
    sizes = (Q_LORA, KV_LORA, MLA_ROPE, MOBA_WIDTH, MOBA_WIDTH, MOBA_WIDTH, D_MODEL, D_MODEL)
    qa, ckv, kr, qm, km, vm, ga, gb = jnp.split(w_in[l], [int(v) for v in np.cumsum(sizes)[:-1]], axis=1)
    pad = jnp.zeros((D_MODEL, ROPE_PAD - MLA_ROPE), F32)
    w_a = jnp.concatenate([qa, ckv, qm, km, vm, kr, pad, _swap_halves(kr), pad], axis=1)
    wq = w_q_b[l].reshape(Q_LORA, MLA_HEADS, MLA_NOPE + MLA_ROPE)
    nope = wq[:, :, :MLA_NOPE].reshape(Q_LORA, MLA_HEADS * MLA_NOPE)
    rope = wq[:, :, MLA_NOPE:]
    padq = ((0, 0), (0, 0), (0, ROPE_PAD - MLA_ROPE))
    rope_p = jnp.pad(rope, padq).reshape(Q_LORA, MLA_HEADS * ROPE_PAD)
    ropes_p = jnp.pad(_swap_halves(rope), padq).reshape(Q_LORA, MLA_HEADS * ROPE_PAD)
    w_qb = jnp.concatenate([nope, rope_p, ropes_p], axis=1)
    wuk_t = jnp.transpose(w_uk[l], (1, 2, 0))
    z = jnp.zeros((MLA_NOPE, KV_LORA), F32)
    pairs = [jnp.concatenate([jnp.concatenate([wuk_t[2 * p], z], axis=1),
                              jnp.concatenate([z, wuk_t[2 * p + 1]], axis=1)], axis=0)
             for p in range(MLA_HEADS // 2)]
    wuv_t = jnp.transpose(w_uv[l], (1, 0, 2))
    wuv_pad = jnp.stack([jnp.pad(wuv_t[h], ((0, 0), (h * MLA_V, (MLA_HEADS - 1 - h) * MLA_V)))
                         for h in range(MLA_HEADS)])
    wub = w_up_b[l].reshape(MOBA_HEADS, MOBA_DH, D_MODEL)
    wub_pad = jnp.pad(wub, ((0, 0), (0, MOBA_PAD - MOBA_DH), (0, 0))).reshape(MOBA_PADW, D_MODEL)
    return {
        "attn_norm": attn_norm[l][None, :], "w_a": w_a.astype(BF16),
        "q_a_norm": q_a_norm[l][None, :], "w_qb": w_qb.astype(BF16),
        "kv_a_norm": kv_a_norm[l][None, :], "w_uk_pairs": jnp.stack(pairs).astype(BF16),
        "wuv_pad": wuv_pad.astype(BF16),
        "w_g": jnp.concatenate([ga, gb], axis=1).astype(BF16),
        "w_up_a": w_up_a[l].astype(BF16), "w_up_b": w_up_b[l].astype(BF16), "w_up_b_pad": wub_pad.astype(BF16),
        "w_o": w_o[l].astype(BF16),
        "ffn_norm": ffn_norm[l][None, :], "peer_wq_t": peer_w_query[l].T.astype(BF16),
        "peer_k1": peer_keys_1[l].astype(BF16), "peer_k2": peer_keys_2[l].astype(BF16),
        "peer_u": peer_u[l].astype(BF16), "peer_vt": peer_v[l].T.astype(BF16),
        "final_norm": final_norm[None, :],
    }


def _token_tile(n, want):
    t = min(want, n)
    assert n % t == 0
    return t


def kernel(x_prompt, x_sample, cache_mla_latent, cache_mla_rope, cache_moba_k, cache_moba_v, page_table,
           attn_norm, w_in, q_a_norm, w_q_b, kv_a_norm, w_uk, w_uv, w_up_a, w_up_b, w_o,
           ffn_norm, peer_w_query, peer_keys_1, peer_keys_2, peer_u, peer_v, final_norm):
    batch, seq, _ = x_prompt.shape
    bd, dec, _ = x_sample.shape
    depth = w_in.shape[0]
    n_pages = page_table.shape[1]
    past = n_pages * PAGE_SIZE
    n_p, n_s = batch * seq, bd * dec
    tm_p = _token_tile(seq, 256)
    tm_s = _token_tile(n_s, 256)
    assert tm_s % dec == 0
    cs_p, sn_p = _rope_tables(jnp.arange(seq))
    cs_s, sn_s = _rope_tables(past + jnp.arange(dec))
    cs_s, sn_s = jnp.tile(cs_s, (tm_s // dec, 1)), jnp.tile(sn_s, (tm_s // dec, 1))
    page_flat = page_table.reshape(-1).astype(jnp.int32)

    assert depth == 1, "the PEER kernel applies the final norm, so it closes the only layer"
    l = 0
    w = _layer_weights(l, attn_norm, w_in, q_a_norm, w_q_b, kv_a_norm, w_uk, w_uv, w_up_a, w_up_b, w_o,
                       ffn_norm, peer_w_query, peer_keys_1, peer_keys_2, peer_u, peer_v, final_norm)
    hp = x_prompt.reshape(n_p, D_MODEL)
    qcat, kcat, c, kr, qm, km, vm, qp, kp, vp = _inproj(hp, cs_p, sn_p, w, tm_p)
    ya = _mla_prompt(qcat, kcat, w["wuv_pad"], batch, seq, _token_tile(seq, 256))
    ob = _moba_prompt(qm, km, qp, kp, vp, batch, seq)
    y_p = _peer(_merge(hp, ya, ob, w["w_up_b_pad"], w, tm_p), w, _token_tile(n_p, 256))
    outs_p = (c.reshape(1, batch, seq, KV_LORA), kr.reshape(1, batch, seq, MLA_ROPE),
              km.reshape(1, batch, seq, MOBA_HEADS, MOBA_DH), vm.reshape(1, batch, seq, MOBA_HEADS, MOBA_DH))
    hs = x_sample.reshape(n_s, D_MODEL)
    qcat, kcat, c, kr, qm, km, vm, _, _, _ = _inproj(hs, cs_s, sn_s, w, tm_s)
    ya = _mla_sample(page_flat, qcat.reshape(bd, dec, -1), kcat.reshape(bd, dec, -1),
                     cache_mla_latent[l], cache_mla_rope[l], w["wuv_pad"], n_pages)
    ob = _moba_sample(page_flat, qm.reshape(bd, dec, -1), km.reshape(bd, dec, -1), vm.reshape(bd, dec, -1),
                      cache_moba_k[l].reshape(-1, PAGE_SIZE, MOBA_WIDTH),
                      cache_moba_v[l].reshape(-1, PAGE_SIZE, MOBA_WIDTH), n_pages)
    y_s = _peer(_merge(hs, ya.reshape(n_s, -1), ob.reshape(n_s, -1), w["w_up_b"], w, tm_s), w,
                _token_tile(n_s, 256))
    outs_s = (c.reshape(1, bd, dec, KV_LORA), kr.reshape(1, bd, dec, MLA_ROPE),
              km.reshape(1, bd, dec, MOBA_HEADS, MOBA_DH), vm.reshape(1, bd, dec, MOBA_HEADS, MOBA_DH))
    return (y_p.reshape(batch, seq, D_MODEL), y_s.reshape(bd, dec, D_MODEL), *outs_p, *outs_s)
```

```python
import functools

import numpy as np
import jax
import jax.numpy as jnp
from jax import lax
from jax.experimental import pallas as pl
from jax.experimental.pallas import tpu as pltpu

F32 = jnp.float32
BF16 = jnp.bfloat16

D_MODEL = 1024
PAGE_SIZE = 128

MLA_HEADS = 8
MLA_NOPE = 64
MLA_ROPE = 32
MLA_V = 64
Q_LORA = 768
KV_LORA = 256
ROPE_THETA = 10000.0
MLA_SCALE = (MLA_NOPE + MLA_ROPE) ** -0.5
ROPE_PAD = 128
QCAT = KV_LORA + ROPE_PAD

MOBA_HEADS = 8
MOBA_DH = 64
MOBA_WIDTH = MOBA_HEADS * MOBA_DH
MOBA_BLOCK = 256
MOBA_TOPK = 3
MOBA_SCALE = MOBA_DH ** -0.5
MOBA_PAD = 128
MOBA_PADW = MOBA_HEADS * MOBA_PAD

PEER_HEADS = 8
PEER_NKEYS = 128
PEER_DKEY = 256
PEER_TOPK = 16
PEER_ECHUNK = 2048
PEER_ICHUNK = PEER_ECHUNK // PEER_NKEYS
PEER_KTILE = 256

RMS_EPS = 1e-6
NEG_INF = -1e30

VMEM_LIMIT_BYTES = 56 * 1024 * 1024


def _cparams(*sem):
    return pltpu.CompilerParams(dimension_semantics=sem, vmem_limit_bytes=VMEM_LIMIT_BYTES)


def _rms(xf, w):
    return xf * lax.rsqrt(jnp.mean(xf * xf, axis=-1, keepdims=True) + RMS_EPS) * w


def _dot_nt(a, b):
    return lax.dot_general(a, b, (((1,), (1,)), ((), ())), preferred_element_type=F32)


def _full_spec(shape):
    n = len(shape)
    return pl.BlockSpec(shape, lambda *_: (0,) * n)


_A_QA = 0
_A_CKV = Q_LORA
_A_QM = _A_CKV + KV_LORA
_A_KM = _A_QM + MOBA_WIDTH
_A_VM = _A_KM + MOBA_WIDTH
_A_KR = _A_VM + MOBA_WIDTH
_A_KRS = _A_KR + ROPE_PAD
_A_WIDTH = _A_KRS + ROPE_PAD
_Q_NOPE = 0
_Q_ROPE = MLA_HEADS * MLA_NOPE
_Q_ROPES = _Q_ROPE + MLA_HEADS * ROPE_PAD
_Q_WIDTH = _Q_ROPES + MLA_HEADS * ROPE_PAD


def _pad_heads(v, lane_lo):
    out = []
    for p in range(MOBA_HEADS // 2):
        blk = v[:, p * 128:(p + 1) * 128]
        out.append(jnp.where(lane_lo, blk, 0.0))
        out.append(jnp.where(lane_lo, pltpu.roll(blk, MOBA_DH, axis=1), 0.0))
    return jnp.concatenate(out, axis=1)


def _inproj_body(x_ref, cs_ref, sn_ref, an_ref, wa_ref, qan_ref, wqb_ref, kvn_ref, wuk_ref,
                 qcat_ref, kcat_ref, c_ref, kr_ref, qm_ref, km_ref, vm_ref, qp_ref, kp_ref, vp_ref):
    xn = _rms(x_ref[...], an_ref[...]).astype(BF16)
    z = jnp.dot(xn, wa_ref[...], preferred_element_type=F32)
    cs = cs_ref[...]
    sn = sn_ref[...]
    qn = _rms(z[:, _A_QA:_A_QA + Q_LORA], qan_ref[...]).astype(BF16)
    q = jnp.dot(qn, wqb_ref[...], preferred_element_type=F32)
    for p in range(MLA_HEADS // 2):
        lat = jnp.dot(q[:, 128 * p:128 * (p + 1)].astype(BF16), wuk_ref[p], preferred_element_type=F32)
        for hl in range(2):
            h = 2 * p + hl
            qcat_ref[:, h * QCAT:h * QCAT + KV_LORA] = (lat[:, hl * KV_LORA:(hl + 1) * KV_LORA] * MLA_SCALE).astype(BF16)
            rope = (q[:, _Q_ROPE + h * ROPE_PAD:_Q_ROPE + (h + 1) * ROPE_PAD] * cs
                    + q[:, _Q_ROPES + h * ROPE_PAD:_Q_ROPES + (h + 1) * ROPE_PAD] * sn)
            qcat_ref[:, h * QCAT + KV_LORA:(h + 1) * QCAT] = (rope * MLA_SCALE).astype(BF16)
    c = _rms(z[:, _A_CKV:_A_CKV + KV_LORA], kvn_ref[...])
    c_ref[...] = c
    kr = z[:, _A_KR:_A_KR + ROPE_PAD] * cs + z[:, _A_KRS:_A_KRS + ROPE_PAD] * sn
    kr_ref[...] = kr[:, :MLA_ROPE]
    kcat_ref[:, :KV_LORA] = c.astype(BF16)
    kcat_ref[:, KV_LORA:] = kr.astype(BF16)
    qm = z[:, _A_QM:_A_QM + MOBA_WIDTH]
    km = z[:, _A_KM:_A_KM + MOBA_WIDTH]
    vm = z[:, _A_VM:_A_VM + MOBA_WIDTH]
    qm_ref[...] = qm
    km_ref[...] = km
    vm_ref[...] = vm
    lane_lo = lax.broadcasted_iota(jnp.int32, (qm.shape[0], 128), 1) < MOBA_DH
    qp_ref[...] = _pad_heads(qm * MOBA_SCALE, lane_lo).astype(BF16)
    kp_ref[...] = _pad_heads(km, lane_lo).astype(BF16)
    vp_ref[...] = _pad_heads(vm, lane_lo).astype(BF16)


def _inproj(x2, cs, sn, w, tm):
    n = x2.shape[0]
    tab_blocks = cs.shape[0] // tm
    row = lambda i: (i, 0)
    tab = lambda i: (i % tab_blocks, 0)
    widths = [(MLA_HEADS * QCAT, BF16), (QCAT, BF16), (KV_LORA, F32), (MLA_ROPE, F32),
              (MOBA_WIDTH, F32), (MOBA_WIDTH, F32), (MOBA_WIDTH, F32),
              (MOBA_PADW, BF16), (MOBA_PADW, BF16), (MOBA_PADW, BF16)]
    return pl.pallas_call(
        _inproj_body,
        grid=(n // tm,),
        in_specs=[pl.BlockSpec((tm, D_MODEL), row),
                  pl.BlockSpec((tm, ROPE_PAD), tab), pl.BlockSpec((tm, ROPE_PAD), tab),
                  _full_spec((1, D_MODEL)), _full_spec((D_MODEL, _A_WIDTH)),
                  _full_spec((1, Q_LORA)), _full_spec((Q_LORA, _Q_WIDTH)),
                  _full_spec((1, KV_LORA)), _full_spec((MLA_HEADS // 2, 128, 2 * KV_LORA))],
        out_specs=[pl.BlockSpec((tm, wd), row) for wd, _ in widths],
        out_shape=[jax.ShapeDtypeStruct((n, wd), dt) for wd, dt in widths],
        compiler_params=_cparams("arbitrary"),
        name="inproj",
    )(x2, cs, sn, w["attn_norm"], w["w_a"], w["q_a_norm"], w["w_qb"], w["kv_a_norm"], w["w_uk_pairs"])


def _softmax_step(s, v, m_scr, l_scr, acc_scr):
    m_prev = m_scr[...]
    m_new = jnp.maximum(m_prev, jnp.max(s, axis=-1, keepdims=True))
    alpha = jnp.exp(m_prev - m_new)
    p = jnp.exp(s - m_new)
    l_scr[...] = alpha * l_scr[...] + jnp.sum(p, axis=-1, keepdims=True)
    acc_scr[...] = alpha * acc_scr[...] + jnp.dot(p.astype(BF16), v, preferred_element_type=F32)
    m_scr[...] = m_new


def _mla_prompt_body(qcat_ref, kcat_ref, wuv_ref, out_ref, q_scr, m_scr, l_scr, acc_scr, *, tq):
    qi = pl.program_id(1)
    for h in range(MLA_HEADS):
        q_scr[h * tq:(h + 1) * tq, :] = qcat_ref[:, h * QCAT:(h + 1) * QCAT]
    m_scr[...] = jnp.full(m_scr.shape, NEG_INF, F32)
    l_scr[...] = jnp.zeros(l_scr.shape, F32)
    acc_scr[...] = jnp.zeros(acc_scr.shape, F32)

    def chunk(kc, masked):
        k = kcat_ref[pl.ds(pl.multiple_of(kc * tq, tq), tq), :]
        s = _dot_nt(q_scr[...], k)
        if masked:
            tok = lax.broadcasted_iota(jnp.int32, s.shape, 0) & (tq - 1)
            col = lax.broadcasted_iota(jnp.int32, s.shape, 1)
            s = jnp.where(col <= tok, s, NEG_INF)
        _softmax_step(s, k[:, :KV_LORA], m_scr, l_scr, acc_scr)

    def past(kc, carry):
        chunk(kc, False)
        return carry

    lax.fori_loop(0, qi, past, 0)
    chunk(qi, True)
    o = (acc_scr[...] / l_scr[...]).astype(BF16)
    y = jnp.zeros(out_ref.shape, F32)
    for h in range(MLA_HEADS):
        y = y + jnp.dot(o[h * tq:(h + 1) * tq], wuv_ref[h], preferred_element_type=F32)
    out_ref[...] = y.astype(out_ref.dtype)


def _mla_prompt(qcat, kcat, wuv_pad, batch, seq, tq):
    nq = seq // tq
    rows = MLA_HEADS * tq
    return pl.pallas_call(
        functools.partial(_mla_prompt_body, tq=tq),
        grid=(batch, nq),
        in_specs=[pl.BlockSpec((tq, MLA_HEADS * QCAT), lambda b, i: (b * nq + i, 0)),
                  pl.BlockSpec((seq, QCAT), lambda b, i: (b, 0)),
                  _full_spec((MLA_HEADS, KV_LORA, MLA_HEADS * MLA_V))],
        out_specs=pl.BlockSpec((tq, MLA_HEADS * MLA_V), lambda b, i: (b * nq + i, 0)),
        out_shape=jax.ShapeDtypeStruct((batch * seq, MLA_HEADS * MLA_V), BF16),
        scratch_shapes=[pltpu.VMEM((rows, QCAT), BF16), pltpu.VMEM((rows, 1), F32),
                        pltpu.VMEM((rows, 1), F32), pltpu.VMEM((rows, KV_LORA), F32)],
        compiler_params=_cparams("arbitrary", "arbitrary"),
        name="mla_prompt",
    )(qcat, kcat, wuv_pad)


def _alibi_slope(h):
    return 2.0 ** (-8.0 * (h + 1) / MOBA_HEADS)


def _block_choice(scores, n_valid):
    lane = lax.broadcasted_iota(jnp.int32, scores.shape, 1)
    n = lane & 7
    s = jnp.where(n < n_valid, scores, NEG_INF)
    beaten = jnp.zeros(scores.shape, F32)
    for d in range(1, 8):
        other = jnp.where(n >= d, pltpu.roll(s, d, axis=1), pltpu.roll(s, 128 - 8 + d, axis=1))
        wins = (other > s) | ((other == s) & (n >= d))
        beaten = beaten + wins.astype(F32)
    return jnp.where((beaten < MOBA_TOPK) & (n < n_valid), 1.0, 0.0)


def _moba_prompt_body(qf_ref, kf_ref, qp_ref, kp_ref, vp_ref, out_ref, kmean_scr, sel_scr, m_scr, l_scr, acc_scr):
    qi = pl.program_id(1)
    j = pl.program_id(2)
    nblk = kmean_scr.shape[0]
    blk = MOBA_BLOCK

    @pl.when((qi == 0) & (j == 0))
    def _():
        kmean_scr[...] = jnp.zeros(kmean_scr.shape, F32)

    @pl.when(j == 0)
    def _():
        m_scr[...] = jnp.full(m_scr.shape, NEG_INF, F32)
        l_scr[...] = jnp.zeros(l_scr.shape, F32)
        acc_scr[...] = jnp.zeros(acc_scr.shape, F32)
        km = kmean_scr[...]
        tiled = jnp.concatenate([km] * MOBA_HEADS + [jnp.zeros((128 - MOBA_HEADS * nblk, MOBA_WIDTH), F32)], axis=0)
        row_h = lax.broadcasted_iota(jnp.int32, tiled.shape, 0) // nblk
        lane_h = lax.broadcasted_iota(jnp.int32, tiled.shape, 1) // MOBA_DH
        kmbd = jnp.where(row_h == lane_h, tiled, 0.0)
        scores = lax.dot_general(qf_ref[...], kmbd, (((1,), (1,)), ((), ())),
                                 preferred_element_type=F32, precision=lax.Precision.HIGHEST)
        sel_scr[...] = _block_choice(scores, qi)

    @pl.when(j <= qi)
    def _():
        diag = (j == qi).astype(jnp.int32)
        r = lax.broadcasted_iota(jnp.int32, (blk, blk), 0)
        c = lax.broadcasted_iota(jnp.int32, (blk, blk), 1)
        dist = ((qi - j) * blk + (r - c)).astype(F32)
        causal = (r >= c).astype(jnp.int32) * diag
        lane = lax.broadcasted_iota(jnp.int32, sel_scr.shape, 1)
        sel = sel_scr[...]
        for h in range(MOBA_HEADS):
            sl = slice(h * MOBA_PAD, (h + 1) * MOBA_PAD)
            s = _dot_nt(qp_ref[:, sl], kp_ref[:, sl])
            s = s - _alibi_slope(h) * dist
            chosen = jnp.sum(jnp.where(lane == h * 8 + j, sel, 0.0), axis=-1, keepdims=True)
            allowed = causal + (chosen > 0.5).astype(jnp.int32) * (1 - diag)
            s = jnp.where(allowed > 0, s, NEG_INF)
            _softmax_step(s, vp_ref[:, sl], m_scr.at[h], l_scr.at[h], acc_scr.at[h])

    @pl.when(j == qi)
    def _():
        kmean_scr[pl.ds(qi, 1), :] = jnp.mean(kf_ref[...], axis=0, keepdims=True)
        for h in range(MOBA_HEADS):
            out_ref[:, h * MOBA_PAD:(h + 1) * MOBA_PAD] = (acc_scr[h] / l_scr[h]).astype(out_ref.dtype)


def _moba_prompt(qm, km, qp, kp, vp, batch, seq):
    nb = seq // MOBA_BLOCK
    assert nb <= 8 and seq % MOBA_BLOCK == 0
    qmap = lambda b, i, j: (b * nb + i, 0)
    kmap = lambda b, i, j: (b * nb + jnp.minimum(i, j), 0)
    blk = MOBA_BLOCK
    return pl.pallas_call(
        _moba_prompt_body,
        grid=(batch, nb, nb),
        in_specs=[pl.BlockSpec((blk, MOBA_WIDTH), qmap), pl.BlockSpec((blk, MOBA_WIDTH), qmap),
                  pl.BlockSpec((blk, MOBA_PADW), qmap),
                  pl.BlockSpec((blk, MOBA_PADW), kmap), pl.BlockSpec((blk, MOBA_PADW), kmap)],
        out_specs=pl.BlockSpec((blk, MOBA_PADW), qmap),
        out_shape=jax.ShapeDtypeStruct((batch * seq, MOBA_PADW), BF16),
        scratch_shapes=[pltpu.VMEM((8, MOBA_WIDTH), F32), pltpu.VMEM((blk, 128), F32),
                        pltpu.VMEM((MOBA_HEADS, blk, 1), F32), pltpu.VMEM((MOBA_HEADS, blk, 1), F32),
                        pltpu.VMEM((MOBA_HEADS, blk, MOBA_PAD), F32)],
        compiler_params=_cparams("arbitrary", "arbitrary", "arbitrary"),
        name="moba_prompt",
    )(qm, km, qp, kp, vp)


def _sigmoid(g):
    return 1.0 / (1.0 + jnp.exp(-g))


def _merge_body(x_ref, ya_ref, ob_ref, an_ref, wg_ref, wua_ref, wub_ref, wo_ref, h_ref):
    x = x_ref[...]
    xn = _rms(x, an_ref[...]).astype(BF16)
    g = jnp.dot(xn, wg_ref[...], preferred_element_type=F32)
    y_a = jnp.dot(ya_ref[...], wua_ref[...], preferred_element_type=F32)
    y_b = jnp.dot(ob_ref[...], wub_ref[...], preferred_element_type=F32)
    mix = _sigmoid(g[:, :D_MODEL]) * y_a + _sigmoid(g[:, D_MODEL:]) * y_b
    h_ref[...] = x + jnp.dot(mix.astype(BF16), wo_ref[...], preferred_element_type=F32)


def _merge(x2, ya, ob, w_up_b, w, tm):
    n = x2.shape[0]
    wb = ob.shape[1]
    row = lambda i: (i, 0)
    return pl.pallas_call(
        _merge_body,
        grid=(n // tm,),
        in_specs=[pl.BlockSpec((tm, D_MODEL), row), pl.BlockSpec((tm, MLA_HEADS * MLA_V), row),
                  pl.BlockSpec((tm, wb), row),
                  _full_spec((1, D_MODEL)), _full_spec((D_MODEL, 2 * D_MODEL)),
                  _full_spec((MLA_HEADS * MLA_V, D_MODEL)), _full_spec((wb, D_MODEL)),
                  _full_spec((D_MODEL, D_MODEL))],
        out_specs=pl.BlockSpec((tm, D_MODEL), row),
        out_shape=jax.ShapeDtypeStruct((n, D_MODEL), F32),
        compiler_params=_cparams("arbitrary"),
        name="merge",
    )(x2, ya, ob, w["attn_norm"], w["w_g"], w["w_up_a"], w_up_b, w["w_o"])


def _topk_ranks(s, key):
    rank = jnp.full(s.shape, float(PEER_TOPK), F32)
    work = s
    vals = []
    for r in range(PEER_TOPK):
        m = jnp.max(work, axis=0, keepdims=True)
        kmin = jnp.min(jnp.where(work == m, key, 1e9), axis=0, keepdims=True)
        hit = key == kmin
        rank = jnp.where(hit, float(r), rank)
        work = jnp.where(hit, -jnp.inf, work)
        vals.append(m)
    return rank, vals


def _gelu(x):
    return 0.5 * x * (1.0 + lax.erf(x * np.float32(np.sqrt(0.5))))


def _peer_body(h_ref, fn_ref, wq_ref, k1_ref, k2_ref, u_ref, vt_ref, on_ref, y_ref,
               xn_scr, qt_scr, r2_scr, e2_scr, nb_scr, p1_scr, acc_scr):
    c = pl.program_id(1)
    t = h_ref.shape[0]
    kk = PEER_TOPK

    @pl.when(c == 0)
    def _():
        xn = _rms(h_ref[...], fn_ref[...]).astype(BF16)
        xn_scr[...] = xn
        qt_scr[...] = _dot_nt(wq_ref[...], xn)
        acc_scr[...] = jnp.zeros(acc_scr.shape, F32)
        key128 = lax.broadcasted_iota(jnp.int32, (PEER_NKEYS, t), 0).astype(F32)
        r16 = lax.broadcasted_iota(jnp.int32, (kk, t), 0).astype(F32)
        r8 = lax.broadcasted_iota(jnp.int32, (8, t), 0).astype(F32)

        def per_head(hd, carry):
            base = pl.multiple_of(hd * PEER_DKEY, PEER_DKEY)
            q1 = qt_scr[pl.ds(base, PEER_DKEY // 2), :].astype(BF16)
            q2 = qt_scr[pl.ds(base + PEER_DKEY // 2, PEER_DKEY // 2), :].astype(BF16)
            s1 = jnp.dot(k1_ref[hd], q1, preferred_element_type=F32)
            s2 = jnp.dot(k2_ref[hd], q2, preferred_element_type=F32)
            rank1, v1r = _topk_ranks(s1, key128)
            rank2, v2r = _topk_ranks(s2, key128)
            v1, v1h = jnp.concatenate(v1r, axis=0), jnp.concatenate(v1r[:8], axis=0)
            v2, v2h = jnp.concatenate(v2r, axis=0), jnp.concatenate(v2r[:8], axis=0)
            slabs = [(v1 + v2r[0], r16 * kk, None),
                     (v1h + v2r[1], r8 * kk + 1.0, None),
                     (v2 + v1r[0], r16, r16 >= 2.0),
                     (v2h + v1r[1], r8 + float(kk), r8 >= 2.0)]
            for b in (2, 3, 4):
                slabs.append((v1h + v2r[b], r8 * kk + float(b), (r8 >= 2.0) & ((r8 + 1.0) * (b + 1) <= kk)))
            cand = jnp.concatenate([vals if ok is None else jnp.where(ok, vals, -jnp.inf) for vals, _, ok in slabs],
                                   axis=0)
            ckey = jnp.concatenate([key if ok is None else jnp.where(ok, key, 1e6) for _, key, ok in slabs], axis=0)
            rankc, _ = _topk_ranks(cand, ckey)
            selc = jnp.where(rankc < float(kk), 1.0, 0.0)
            sel_a, sel_b, sel_c, sel_d = selc[0:16], selc[16:24], selc[24:40], selc[40:48]
            sel_e = [selc[48:56], selc[56:64], selc[64:72]]
            e1, e1h = jnp.exp(v1 - v1r[0]), jnp.exp(v1h - v1r[0])
            e2, e2h = jnp.exp(v2 - v2r[0]), jnp.exp(v2h - v2r[0])
            e1r = [jnp.exp(v1r[a] - v1r[0]) for a in range(kk)]
            e2r = [jnp.exp(v2r[b] - v2r[0]) for b in range(kk)]
            colsum = lambda x: jnp.sum(x, axis=0, keepdims=True)
            zero8 = jnp.zeros((8, t), F32)
            nb = (sel_a + jnp.concatenate([sel_b + sel_e[0] + sel_e[1] + sel_e[2], zero8], axis=0)
                  + jnp.where(r16 == 0.0, colsum(sel_c), 0.0) + jnp.where(r16 == 1.0, colsum(sel_d), 0.0))
            zsum = (colsum(sel_a * e1) * e2r[0] + colsum(sel_b * e1h) * e2r[1]
                    + colsum(sel_c * e2) * e1r[0] + colsum(sel_d * e2h) * e1r[1])
            for n, b in enumerate((2, 3, 4)):
                zsum = zsum + colsum(sel_e[n] * e1h) * e2r[b]
            e2r = [e / zsum for e in e2r]
            nbx = jnp.zeros((PEER_NKEYS, t), F32)
            p1x = jnp.zeros((PEER_NKEYS, t), F32)
            e2x = jnp.zeros((PEER_NKEYS, t), F32)
            for a in range(kk):
                hit1 = rank1 == float(a)
                nbx = jnp.where(hit1, nb[a:a + 1], nbx)
                p1x = jnp.where(hit1, e1r[a], p1x)
                e2x = jnp.where(rank2 == float(a), e2r[a], e2x)
            r2_scr[hd] = rank2
            e2_scr[hd] = e2x
            nb_scr[hd] = nbx
            p1_scr[hd] = p1x
            return carry

        lax.fori_loop(0, PEER_HEADS, per_head, 0)

    xn = xn_scr[...]
    acc = None
    for kt in range(PEER_ECHUNK // PEER_KTILE):
        rows = slice(kt * PEER_KTILE, (kt + 1) * PEER_KTILE)
        act = _dot_nt(u_ref[rows, :], xn)
        parts = []
        for ii in range(PEER_KTILE // PEER_NKEYS):
            i = c * PEER_ICHUNK + kt * (PEER_KTILE // PEER_NKEYS) + ii
            wgt = jnp.zeros((PEER_NKEYS, t), F32)
            for hd in range(PEER_HEADS):
                nbrow = nb_scr[hd, pl.ds(i, 1), :]
                p1row = p1_scr[hd, pl.ds(i, 1), :]
                wgt = wgt + jnp.where(r2_scr[hd] < nbrow, e2_scr[hd] * p1row, 0.0)
            parts.append((wgt * _gelu(act[ii * PEER_NKEYS:(ii + 1) * PEER_NKEYS])).astype(BF16))
        piece = jnp.dot(vt_ref[:, rows], jnp.concatenate(parts, axis=0), preferred_element_type=F32)
        acc = piece if acc is None else acc + piece
    acc_scr[...] += acc

    @pl.when(c == pl.num_programs(1) - 1)
    def _():
        h2 = h_ref[...] + acc_scr[...].T
        y_ref[...] = _rms(h2, on_ref[...])


def _peer(h2d, w, t):
    n = h2d.shape[0]
    n_exp = w["peer_u"].shape[0]
    nchunk = n_exp // PEER_ECHUNK
    slab = (PEER_HEADS, PEER_NKEYS, t)
    return pl.pallas_call(
        _peer_body,
        grid=(n // t, nchunk),
        in_specs=[pl.BlockSpec((t, D_MODEL), lambda i, c: (i, 0)),
                  _full_spec((1, D_MODEL)), _full_spec((PEER_HEADS * PEER_DKEY, D_MODEL)),
                  _full_spec((PEER_HEADS, PEER_NKEYS, PEER_DKEY // 2)),
                  _full_spec((PEER_HEADS, PEER_NKEYS, PEER_DKEY // 2)),
                  pl.BlockSpec((PEER_ECHUNK, D_MODEL), lambda i, c: (c, 0)),
                  pl.BlockSpec((D_MODEL, PEER_ECHUNK), lambda i, c: (0, c)),
                  _full_spec((1, D_MODEL))],
        out_specs=pl.BlockSpec((t, D_MODEL), lambda i, c: (i, 0)),
        out_shape=jax.ShapeDtypeStruct((n, D_MODEL), F32),
        scratch_shapes=[pltpu.VMEM((t, D_MODEL), BF16), pltpu.VMEM((PEER_HEADS * PEER_DKEY, t), F32),
                        pltpu.VMEM(slab, F32), pltpu.VMEM(slab, F32), pltpu.VMEM(slab, F32), pltpu.VMEM(slab, F32),
                        pltpu.VMEM((D_MODEL, t), F32)],
        compiler_params=_cparams("arbitrary", "arbitrary"),
        name="peer",
    )(h2d, w["ffn_norm"], w["peer_wq_t"], w["peer_k1"], w["peer_k2"], w["peer_u"], w["peer_vt"], w["final_norm"])


SAMPLE_PAGES_PER_STEP = 8


def _mla_sample_body(pt_ref, q_ref, knew_ref, *rest, pages, dec):
    c_refs = rest[:pages]
    r_refs = rest[pages:2 * pages]
    wuv_ref, out_ref, q_scr, knew_scr, m_scr, l_scr, acc_scr = rest[2 * pages:]
    s_id = pl.program_id(1)
    rows = MLA_HEADS * dec

    @pl.when(s_id == 0)
    def _():
        for h in range(MLA_HEADS):
            q_scr[h * dec:(h + 1) * dec, :] = q_ref[0, :, h * QCAT:(h + 1) * QCAT].astype(F32)
        m_scr[...] = jnp.full(m_scr.shape, NEG_INF, F32)
        l_scr[...] = jnp.zeros(l_scr.shape, F32)
        acc_scr[...] = jnp.zeros(acc_scr.shape, F32)

    q = q_scr[...].astype(BF16)
    q_lat = q[:, :KV_LORA]
    q_rope = q[:, KV_LORA:KV_LORA + MLA_ROPE]
    cs = [c_refs[i][0].astype(BF16) for i in range(pages)]
    s = jnp.concatenate([_dot_nt(q_lat, cs[i])
                         + jnp.dot(q_rope, r_refs[i][0].astype(BF16), preferred_element_type=F32)
                         for i in range(pages)], axis=1)
    m_prev = m_scr[...]
    m_new = jnp.maximum(m_prev, jnp.max(s, axis=-1, keepdims=True))
    alpha = jnp.exp(m_prev - m_new)
    p = jnp.exp(s - m_new)
    l_scr[...] = alpha * l_scr[...] + jnp.sum(p, axis=-1, keepdims=True)
    p = p.astype(BF16)
    pv = jnp.zeros(acc_scr.shape, F32)
    for i in range(pages):
        pv = pv + jnp.dot(p[:, i * PAGE_SIZE:(i + 1) * PAGE_SIZE], cs[i], preferred_element_type=F32)
    acc_scr[...] = alpha * acc_scr[...] + pv
    m_scr[...] = m_new

    @pl.when(s_id == pl.num_programs(1) - 1)
    def _():
        knew_scr[...] = jnp.zeros(knew_scr.shape, F32)
        knew_scr[0:dec, :] = knew_ref[0].astype(F32)
        k = knew_scr[...].astype(BF16)
        s_new = _dot_nt(q, k)
        tok = lax.broadcasted_iota(jnp.int32, s_new.shape, 0) & (dec - 1)
        col = lax.broadcasted_iota(jnp.int32, s_new.shape, 1)
        s_new = jnp.where(col <= tok, s_new, NEG_INF)
        _softmax_step(s_new, k[:, :KV_LORA], m_scr, l_scr, acc_scr)
        o = (acc_scr[...] / l_scr[...]).astype(BF16)
        y = jnp.zeros((dec, MLA_HEADS * MLA_V), F32)
        for h in range(MLA_HEADS):
            y = y + jnp.dot(o[h * dec:(h + 1) * dec], wuv_ref[h], preferred_element_type=F32)
        out_ref[0] = y.astype(out_ref.dtype)


def _mla_sample(page_flat, qcat3, kcat3, cache_c, cache_r, wuv_pad, n_pages):
    bd, dec, _ = qcat3.shape
    assert dec & (dec - 1) == 0 and dec <= PAGE_SIZE
    pages = min(SAMPLE_PAGES_PER_STEP, n_pages)
    steps = n_pages // pages
    rows = MLA_HEADS * dec

    def page_map(i):
        return lambda b, s, pt: (pt[b * n_pages + s * pages + i], 0, 0)

    per_b = lambda b, s, pt: (b, 0, 0)
    grid_spec = pltpu.PrefetchScalarGridSpec(
        num_scalar_prefetch=1,
        grid=(bd, steps),
        in_specs=([pl.BlockSpec((1, dec, MLA_HEADS * QCAT), per_b), pl.BlockSpec((1, dec, QCAT), per_b)]
                  + [pl.BlockSpec((1, PAGE_SIZE, KV_LORA), page_map(i)) for i in range(pages)]
                  + [pl.BlockSpec((1, MLA_ROPE, PAGE_SIZE), page_map(i)) for i in range(pages)]
                  + [pl.BlockSpec((MLA_HEADS, KV_LORA, MLA_HEADS * MLA_V), lambda b, s, pt: (0, 0, 0))]),
        out_specs=pl.BlockSpec((1, dec, MLA_HEADS * MLA_V), per_b),
        scratch_shapes=[pltpu.VMEM((rows, QCAT), F32), pltpu.VMEM((PAGE_SIZE, QCAT), F32),
                        pltpu.VMEM((rows, 1), F32), pltpu.VMEM((rows, 1), F32), pltpu.VMEM((rows, KV_LORA), F32)],
    )
    return pl.pallas_call(
        functools.partial(_mla_sample_body, pages=pages, dec=dec),
        grid_spec=grid_spec,
        out_shape=jax.ShapeDtypeStruct((bd, dec, MLA_HEADS * MLA_V), BF16),
        compiler_params=_cparams("arbitrary", "arbitrary"),
        name="mla_sample",
    )(page_flat, qcat3, kcat3, *([cache_c] * pages), *([cache_r] * pages), wuv_pad)


def _group_choice(scores, group, n_valid, topk):
    lane = lax.broadcasted_iota(jnp.int32, scores.shape, 1)
    s = jnp.where(lane < n_valid, scores, NEG_INF)
    beaten = jnp.zeros(scores.shape, F32)
    for d in range(1, group):
        other = jnp.where(lane >= d, pltpu.roll(s, d, axis=1), pltpu.roll(s, 128 - group + d, axis=1))
        wins = (other > s) | ((other == s) & (lane >= d))
        beaten = beaten + wins.astype(F32)
    return jnp.where((beaten < topk) & (lane < n_valid), 1.0, 0.0)


def _moba_sample_body(pt_ref, qm_ref, kn_ref, vn_ref, e_ref, *rest, pages, dec, n_pages):
    k_refs = rest[:pages]
    v_refs = rest[pages:2 * pages]
    out_ref, qbd_scr, kmt_scr, s_scr, snew_scr, new_scr, l_scr, acc_scr = rest[2 * pages:]
    s_id = pl.program_id(1)
    n_k = n_pages // pages
    rows = MOBA_HEADS * dec
    n_blocks = n_pages // 2
    past = n_pages * PAGE_SIZE
    row = lax.broadcasted_iota(jnp.int32, (rows, PAGE_SIZE), 0)
    col = lax.broadcasted_iota(jnp.int32, (rows, PAGE_SIZE), 1)
    tok = row & (dec - 1)
    slope = jnp.zeros((rows, PAGE_SIZE), F32)
    for h in range(MOBA_HEADS):
        slope = jnp.where(row // dec == h, _alibi_slope(h), slope)

    @pl.when(s_id == 0)
    def _():
        tiled = jnp.concatenate([qm_ref[0]] * MOBA_HEADS, axis=0)
        row_h = lax.broadcasted_iota(jnp.int32, tiled.shape, 0) // dec
        lane_h = lax.broadcasted_iota(jnp.int32, tiled.shape, 1) // MOBA_DH
        qbd_scr[...] = jnp.where(row_h == lane_h, tiled * MOBA_SCALE, 0.0)
        kmt_scr[...] = jnp.zeros(kmt_scr.shape, F32)

    @pl.when(s_id < n_k)
    def _():
        qbd = qbd_scr[...].astype(BF16)
        lane = lax.broadcasted_iota(jnp.int32, kmt_scr.shape, 1)
        for i in range(0, pages, 2):
            kt0 = k_refs[i][0]
            kt1 = k_refs[i + 1][0]
            page = s_id * pages + i
            s_scr[page] = jnp.dot(qbd, kt0.astype(BF16), preferred_element_type=F32)
            s_scr[page + 1] = jnp.dot(qbd, kt1.astype(BF16), preferred_element_type=F32)
            mean = 0.5 * (jnp.mean(kt0, axis=1, keepdims=True) + jnp.mean(kt1, axis=1, keepdims=True))
            kmt_scr[...] = jnp.where(lane == page // 2, mean, kmt_scr[...])

    @pl.when(s_id == n_k)
    def _():
        scores = jnp.dot(qbd_scr[...], kmt_scr[...], preferred_element_type=F32, precision=lax.Precision.HIGHEST)
        sel = _group_choice(scores, max(n_blocks, 1), n_blocks, MOBA_TOPK)
        chosen = jnp.dot(sel.astype(BF16), e_ref[...], preferred_element_type=F32)
        new_scr[...] = jnp.zeros(new_scr.shape, F32)
        new_scr[0:dec, :] = kn_ref[0]
        s_new = _dot_nt(qbd_scr[...].astype(BF16), new_scr[...].astype(BF16))
        s_new = s_new - slope * (tok - col).astype(F32)
        s_new = jnp.where(col <= tok, s_new, NEG_INF)
        bias = slope * (past + tok - col).astype(F32)
        mx = s_new
        for page in range(n_pages):
            cols = slice(page * PAGE_SIZE, (page + 1) * PAGE_SIZE)
            sm = jnp.where(chosen[:, cols] > 0.5, s_scr[page] - (bias - slope * float(page * PAGE_SIZE)), NEG_INF)
            s_scr[page] = sm
            mx = jnp.maximum(mx, sm)
        m = jnp.max(mx, axis=-1, keepdims=True)
        p_new = jnp.exp(s_new - m)
        snew_scr[...] = p_new
        lsum = p_new
        for page in range(n_pages):
            p = jnp.exp(s_scr[page] - m)
            s_scr[page] = p
            lsum = lsum + p
        l_scr[...] = jnp.sum(lsum, axis=-1, keepdims=True)
        acc_scr[...] = jnp.zeros(acc_scr.shape, F32)

    @pl.when(s_id >= n_k)
    def _():
        pv = jnp.zeros(acc_scr.shape, F32)
        for i in range(pages):
            page = (s_id - n_k) * pages + i
            pv = pv + _dot_nt(s_scr[page].astype(BF16), v_refs[i][0].astype(BF16))
        acc_scr[...] += pv

    @pl.when(s_id == 2 * n_k - 1)
    def _():
        new_scr[...] = jnp.zeros(new_scr.shape, F32)
        new_scr[0:dec, :] = vn_ref[0]
        acc = acc_scr[...] + jnp.dot(snew_scr[...].astype(BF16), new_scr[...].astype(BF16),
                                     preferred_element_type=F32)
        o = acc / l_scr[...]
        row_h = lax.broadcasted_iota(jnp.int32, o.shape, 0) // dec
        lane_h = lax.broadcasted_iota(jnp.int32, o.shape, 1) // MOBA_DH
        o = jnp.where(row_h == lane_h, o, 0.0)
        y = o[0:dec]
        for h in range(1, MOBA_HEADS):
            y = y + o[h * dec:(h + 1) * dec]
        out_ref[0] = y.astype(out_ref.dtype)


def _moba_sample(page_flat, qm3, km3, vm3, cache_kt, cache_vt, n_pages):
    bd, dec, _ = qm3.shape
    assert dec == 8, "row groups of the stacked heads must be one sublane tile"
    assert n_pages % 2 == 0, "the past must be a whole number of MoBA blocks"
    n_blocks = n_pages // 2
    assert n_blocks & (n_blocks - 1) == 0 and n_blocks <= 128
    pages = min(SAMPLE_PAGES_PER_STEP, n_pages)
    n_k = n_pages // pages
    rows = MOBA_HEADS * dec

    def k_map(i):
        return lambda b, s, pt: (pt[b * n_pages + jnp.minimum(s, n_k - 1) * pages + i], 0, 0)

    def v_map(i):
        return lambda b, s, pt: (pt[b * n_pages + jnp.maximum(s - n_k, 0) * pages + i], 0, 0)

    per_b = lambda b, s, pt: (b, 0, 0)
    n_keys = n_pages * PAGE_SIZE
    member = (jnp.arange(n_keys)[None, :] // MOBA_BLOCK == jnp.arange(128)[:, None]).astype(BF16)
    grid_spec = pltpu.PrefetchScalarGridSpec(
        num_scalar_prefetch=1,
        grid=(bd, 2 * n_k),
        in_specs=([pl.BlockSpec((1, dec, MOBA_WIDTH), per_b)] * 3
                  + [pl.BlockSpec((128, n_keys), lambda b, s, pt: (0, 0))]
                  + [pl.BlockSpec((1, MOBA_WIDTH, PAGE_SIZE), k_map(i)) for i in range(pages)]
                  + [pl.BlockSpec((1, MOBA_WIDTH, PAGE_SIZE), v_map(i)) for i in range(pages)]),
        out_specs=pl.BlockSpec((1, dec, MOBA_WIDTH), per_b),
        scratch_shapes=[pltpu.VMEM((rows, MOBA_WIDTH), F32), pltpu.VMEM((MOBA_WIDTH, 128), F32),
                        pltpu.VMEM((n_pages, rows, PAGE_SIZE), F32), pltpu.VMEM((rows, PAGE_SIZE), F32),
                        pltpu.VMEM((PAGE_SIZE, MOBA_WIDTH), F32),
                        pltpu.VMEM((rows, 1), F32), pltpu.VMEM((rows, MOBA_WIDTH), F32)],
    )
    return pl.pallas_call(
        functools.partial(_moba_sample_body, pages=pages, dec=dec, n_pages=n_pages),
        grid_spec=grid_spec,
        out_shape=jax.ShapeDtypeStruct((bd, dec, MOBA_WIDTH), BF16),
        compiler_params=_cparams("arbitrary", "arbitrary"),
        name="moba_sample",
    )(page_flat, qm3, km3, vm3, member, *([cache_kt] * pages), *([cache_vt] * pages))


def _rope_tables(pos):
    half = MLA_ROPE // 2
    inv = ROPE_THETA ** (-jnp.arange(half, dtype=F32) / half)
    ang = pos.astype(F32)[:, None] * inv
    cos, sin = jnp.cos(ang), jnp.sin(ang)
    pad = jnp.zeros((pos.shape[0], ROPE_PAD - MLA_ROPE), F32)
    return jnp.concatenate([cos, cos, pad], axis=1), jnp.concatenate([-sin, sin, pad], axis=1)


def _swap_halves(w):
    half = MLA_ROPE // 2
    return jnp.concatenate([w[..., half:], w[..., :half]], axis=-1)


def _layer_weights(l, attn_norm, w_in, q_a_norm, w_q_b, kv_a_norm, w_uk, w_uv, w_up_a, w_up_b, w_o,
                   ffn_norm, peer_w_query, peer_keys_1, peer_keys_2, peer_u, peer_v, final_norm):
    sizes = (Q_LORA, KV_LORA, MLA_ROPE, MOBA_WIDTH, MOBA_WIDTH, MOBA_WIDTH, D_MODEL, D_MODEL)
    qa, ckv, kr, qm, km, vm, ga, gb = jnp.split(w_in[l], [int(v) for v in np.cumsum(sizes)[:-1]], axis=1)
    pad = jnp.zeros((D_MODEL, ROPE_PAD - MLA_ROPE), F32)
    w_a = jnp.concatenate([qa, ckv, qm, km, vm, kr, pad, _swap_halves(kr), pad], axis=1)
    wq = w_q_b[l].reshape(Q_LORA, MLA_HEADS, MLA_NOPE + MLA_ROPE)
    nope = wq[:, :, :MLA_NOPE].reshape(Q_LORA, MLA_HEADS * MLA_NOPE)
    rope = wq[:, :, MLA_NOPE:]
    padq = ((0, 0), (0, 0), (0, ROPE_PAD - MLA_ROPE))
    rope_p = jnp.pad(rope, padq).reshape(Q_LORA, MLA_HEADS * ROPE_PAD)
    ropes_p = jnp.pad(_swap_halves(rope), padq).reshape(Q_LORA, MLA_HEADS * ROPE_PAD)
    w_qb = jnp.concatenate([nope, rope_p, ropes_p], axis=1)
    wuk_t = jnp.transpose(w_uk[l], (1, 2, 0))
    z = jnp.zeros((MLA_NOPE, KV_LORA), F32)
    pairs = [jnp.concatenate([jnp.concatenate([wuk_t[2 * p], z], axis=1),
                              jnp.concatenate([z, wuk_t[2 * p + 1]], axis=1)], axis=0)
             for p in range(MLA_HEADS // 2)]
    wuv_t = jnp.transpose(w_uv[l], (1, 0, 2))
    wuv_pad = jnp.stack([jnp.pad(wuv_t[h], ((0, 0), (h * MLA_V, (MLA_HEADS - 1 - h) * MLA_V)))
                         for h in range(MLA_HEADS)])
    wub = w_up_b[l].reshape(MOBA_HEADS, MOBA_DH, D_MODEL)
    wub_pad = jnp.pad(wub, ((0, 0), (0, MOBA_PAD - MOBA_DH), (0, 0))).reshape(MOBA_PADW, D_MODEL)
    return {
        "attn_norm": attn_norm[l][None, :], "w_a": w_a.astype(BF16),
        "q_a_norm": q_a_norm[l][None, :], "w_qb": w_qb.astype(BF16),
        "kv_a_norm": kv_a_norm[l][None, :], "w_uk_pairs": jnp.stack(pairs).astype(BF16),
        "wuv_pad": wuv_pad.astype(BF16),
        "w_g": jnp.concatenate([ga, gb], axis=1).astype(BF16),
        "w_up_a": w_up_a[l].astype(BF16), "w_up_b": w_up_b[l].astype(BF16), "w_up_b_pad": wub_pad.astype(BF16),
        "w_o": w_o[l].astype(BF16),
        "ffn_norm": ffn_norm[l][None, :], "peer_wq_t": peer_w_query[l].T.astype(BF16),
        "peer_k1": peer_keys_1[l].astype(BF16), "peer_k2": peer_keys_2[l].astype(BF16),
        "peer_u": peer_u[l].astype(BF16), "peer_vt": peer_v[l].T.astype(BF16),
        "final_norm": final_norm[None, :],
    }


def _token_tile(n, want):
    t = min(want, n)
    assert n % t == 0
    return t


def kernel(x_prompt, x_sample, cache_mla_latent, cache_mla_rope, cache_moba_k, cache_moba_v, page_table,
           attn_norm, w_in, q_a_norm, w_q_b, kv_a_norm, w_uk, w_uv, w_up_a, w_up_b, w_o,
           ffn_norm, peer_w_query, peer_keys_1, peer_keys_2, peer_u, peer_v, final_norm):
    batch, seq, _ = x_prompt.shape
    bd, dec, _ = x_sample.shape
    depth = w_in.shape[0]
    n_pages = page_table.shape[1]
    past = n_pages * PAGE_SIZE
    n_p, n_s = batch * seq, bd * dec
    tm_p = _token_tile(seq, 256)
    tm_s = _token_tile(n_s, 256)
    assert tm_s % dec == 0
    cs_p, sn_p = _rope_tables(jnp.arange(seq))
    cs_s, sn_s = _rope_tables(past + jnp.arange(dec))
    cs_s, sn_s = jnp.tile(cs_s, (tm_s // dec, 1)), jnp.tile(sn_s, (tm_s // dec, 1))
    page_flat = page_table.reshape(-1).astype(jnp.int32)

    assert depth == 1, "the PEER kernel applies the final norm, so it closes the only layer"
    l = 0
    w = _layer_weights(l, attn_norm, w_in, q_a_norm, w_q_b, kv_a_norm, w_uk, w_uv, w_up_a, w_up_b, w_o,
                       ffn_norm, peer_w_query, peer_keys_1, peer_keys_2, peer_u, peer_v, final_norm)
    hp = x_prompt.reshape(n_p, D_MODEL)
    qcat, kcat, c, kr, qm, km, vm, qp, kp, vp = _inproj(hp, cs_p, sn_p, w, tm_p)
    ya = _mla_prompt(qcat, kcat, w["wuv_pad"], batch, seq, _token_tile(seq, 256))
    ob = _moba_prompt(qm, km, qp, kp, vp, batch, seq)
    y_p = _peer(_merge(hp, ya, ob, w["w_up_b_pad"], w, tm_p), w, _token_tile(n_p, 256))
    outs_p = (c.reshape(1, batch, seq, KV_LORA), kr.reshape(1, batch, seq, MLA_ROPE),
              km.reshape(1, batch, seq, MOBA_HEADS, MOBA_DH), vm.reshape(1, batch, seq, MOBA_HEADS, MOBA_DH))
    hs = x_sample.reshape(n_s, D_MODEL)
    qcat, kcat, c, kr, qm, km, vm, _, _, _ = _inproj(hs, cs_s, sn_s, w, tm_s)
    rope_t = jnp.swapaxes(cache_mla_rope[l], 1, 2)
    moba_kt = jnp.transpose(cache_moba_k[l], (0, 2, 3, 1)).reshape(-1, MOBA_WIDTH, PAGE_SIZE)
    moba_vt = jnp.transpose(cache_moba_v[l], (0, 2, 3, 1)).reshape(-1, MOBA_WIDTH, PAGE_SIZE)
    ya = _mla_sample(page_flat, qcat.reshape(bd, dec, -1), kcat.reshape(bd, dec, -1),
                     cache_mla_latent[l], rope_t, w["wuv_pad"], n_pages)
    ob = _moba_sample(page_flat, qm.reshape(bd, dec, -1), km.reshape(bd, dec, -1), vm.reshape(bd, dec, -1),
                      moba_kt, moba_vt, n_pages)
    y_s = _peer(_merge(hs, ya.reshape(n_s, -1), ob.reshape(n_s, -1), w["w_up_b"], w, tm_s), w,
                _token_tile(n_s, 256))
    outs_s = (c.reshape(1, bd, dec, KV_LORA), kr.reshape(1, bd, dec, MLA_ROPE),
              km.reshape(1, bd, dec, MOBA_HEADS, MOBA_DH), vm.reshape(1, bd, dec, MOBA_HEADS, MOBA_DH))
    return (y_p.reshape(batch, seq, D_MODEL), y_s.reshape(bd, dec, D_MODEL), *outs_p, *outs_s)
```

```python
import functools

import numpy as np
import jax
import jax.numpy as jnp
from jax import lax
from jax.experimental import pallas as pl
from jax.experimental.pallas import tpu as pltpu

F32 = jnp.float32
BF16 = jnp.bfloat16

D_MODEL = 1024
PAGE_SIZE = 128

MLA_HEADS = 8
MLA_NOPE = 64
MLA_ROPE = 32
MLA_V = 64
Q_LORA = 768
KV_LORA = 256
ROPE_THETA = 10000.0
MLA_SCALE = (MLA_NOPE + MLA_ROPE) ** -0.5
ROPE_PAD = 128
QCAT = KV_LORA + ROPE_PAD

MOBA_HEADS = 8
MOBA_DH = 64
MOBA_WIDTH = MOBA_HEADS * MOBA_DH
MOBA_BLOCK = 256
MOBA_TOPK = 3
MOBA_SCALE = MOBA_DH ** -0.5
MOBA_PAD = 128
MOBA_PADW = MOBA_HEADS * MOBA_PAD

PEER_HEADS = 8
PEER_NKEYS = 128
PEER_DKEY = 256
PEER_TOPK = 16
PEER_ECHUNK = 2048
PEER_ICHUNK = PEER_ECHUNK // PEER_NKEYS
PEER_KTILE = 256

RMS_EPS = 1e-6
NEG_INF = -1e30

VMEM_LIMIT_BYTES = 56 * 1024 * 1024


def _cparams(*sem):
    return pltpu.CompilerParams(dimension_semantics=sem, vmem_limit_bytes=VMEM_LIMIT_BYTES)


def _rms(xf, w):
    return xf * lax.rsqrt(jnp.mean(xf * xf, axis=-1, keepdims=True) + RMS_EPS) * w


def _dot_nt(a, b):
    return lax.dot_general(a, b, (((1,), (1,)), ((), ())), preferred_element_type=F32)


def _full_spec(shape):
    n = len(shape)
    return pl.BlockSpec(shape, lambda *_: (0,) * n)


_A_QA = 0
_A_CKV = Q_LORA
_A_QM = _A_CKV + KV_LORA
_A_KM = _A_QM + MOBA_WIDTH
_A_VM = _A_KM + MOBA_WIDTH
_A_KR = _A_VM + MOBA_WIDTH
_A_KRS = _A_KR + ROPE_PAD
_A_WIDTH = _A_KRS + ROPE_PAD
_Q_NOPE = 0
_Q_ROPE = MLA_HEADS * MLA_NOPE
_Q_ROPES = _Q_ROPE + MLA_HEADS * ROPE_PAD
_Q_WIDTH = _Q_ROPES + MLA_HEADS * ROPE_PAD


def _pad_heads(v, lane_lo, fill):
    out = []
    for p in range(MOBA_HEADS // 2):
        blk = v[:, p * 128:(p + 1) * 128]
        out.append(jnp.where(lane_lo, blk, fill))
        out.append(jnp.where(lane_lo, pltpu.roll(blk, MOBA_DH, axis=1), fill))
    return jnp.concatenate(out, axis=1)


def _inproj_body(x_ref, cs_ref, sn_ref, an_ref, wa_ref, qan_ref, wqb_ref, kvn_ref, wuk_ref,
                 qcat_ref, kcat_ref, c_ref, kr_ref, qm_ref, km_ref, vm_ref, qp_ref, kp_ref, vp_ref):
    xn = _rms(x_ref[...], an_ref[...]).astype(BF16)
    z = jnp.dot(xn, wa_ref[...], preferred_element_type=F32)
    cs = cs_ref[...]
    sn = sn_ref[...]
    qn = _rms(z[:, _A_QA:_A_QA + Q_LORA], qan_ref[...]).astype(BF16)
    q = jnp.dot(qn, wqb_ref[...], preferred_element_type=F32)
    for p in range(MLA_HEADS // 2):
        lat = jnp.dot(q[:, 128 * p:128 * (p + 1)].astype(BF16), wuk_ref[p], preferred_element_type=F32)
        for hl in range(2):
            h = 2 * p + hl
            qcat_ref[:, h * QCAT:h * QCAT + KV_LORA] = (lat[:, hl * KV_LORA:(hl + 1) * KV_LORA] * MLA_SCALE).astype(BF16)
            rope = (q[:, _Q_ROPE + h * ROPE_PAD:_Q_ROPE + (h + 1) * ROPE_PAD] * cs
                    + q[:, _Q_ROPES + h * ROPE_PAD:_Q_ROPES + (h + 1) * ROPE_PAD] * sn)
            qcat_ref[:, h * QCAT + KV_LORA:(h + 1) * QCAT] = (rope * MLA_SCALE).astype(BF16)
    c = _rms(z[:, _A_CKV:_A_CKV + KV_LORA], kvn_ref[...])
    c_ref[...] = c
    kr = z[:, _A_KR:_A_KR + ROPE_PAD] * cs + z[:, _A_KRS:_A_KRS + ROPE_PAD] * sn
    kr_ref[...] = kr[:, :MLA_ROPE]
    kcat_ref[:, :KV_LORA] = c.astype(BF16)
    ones_lane = lax.broadcasted_iota(jnp.int32, kr.shape, 1) == ROPE_PAD - 1
    kcat_ref[:, KV_LORA:] = jnp.where(ones_lane, 1.0, kr).astype(BF16)
    qm = z[:, _A_QM:_A_QM + MOBA_WIDTH]
    km = z[:, _A_KM:_A_KM + MOBA_WIDTH]
    vm = z[:, _A_VM:_A_VM + MOBA_WIDTH]
    qm_ref[...] = qm
    km_ref[...] = km
    vm_ref[...] = vm
    lane = lax.broadcasted_iota(jnp.int32, (qm.shape[0], 128), 1)
    lane_lo = lane < MOBA_DH
    zeros = jnp.zeros(lane.shape, F32)
    qp_ref[...] = _pad_heads(qm * MOBA_SCALE, lane_lo, zeros).astype(BF16)
    kp_ref[...] = _pad_heads(km, lane_lo, zeros).astype(BF16)
    vp_ref[...] = _pad_heads(vm, lane_lo, jnp.where(lane == MOBA_PAD - 1, 1.0, 0.0)).astype(BF16)


def _inproj(x2, cs, sn, w, tm):
    n = x2.shape[0]
    tab_blocks = cs.shape[0] // tm
    row = lambda i: (i, 0)
    tab = lambda i: (i % tab_blocks, 0)
    widths = [(MLA_HEADS * QCAT, BF16), (QCAT, BF16), (KV_LORA, F32), (MLA_ROPE, F32),
              (MOBA_WIDTH, F32), (MOBA_WIDTH, F32), (MOBA_WIDTH, F32),
              (MOBA_PADW, BF16), (MOBA_PADW, BF16), (MOBA_PADW, BF16)]
    return pl.pallas_call(
        _inproj_body,
        grid=(n // tm,),
        in_specs=[pl.BlockSpec((tm, D_MODEL), row),
                  pl.BlockSpec((tm, ROPE_PAD), tab), pl.BlockSpec((tm, ROPE_PAD), tab),
                  _full_spec((1, D_MODEL)), _full_spec((D_MODEL, _A_WIDTH)),
                  _full_spec((1, Q_LORA)), _full_spec((Q_LORA, _Q_WIDTH)),
                  _full_spec((1, KV_LORA)), _full_spec((MLA_HEADS // 2, 128, 2 * KV_LORA))],
        out_specs=[pl.BlockSpec((tm, wd), row) for wd, _ in widths],
        out_shape=[jax.ShapeDtypeStruct((n, wd), dt) for wd, dt in widths],
        compiler_params=_cparams("arbitrary"),
        name="inproj",
    )(x2, cs, sn, w["attn_norm"], w["w_a"], w["q_a_norm"], w["w_qb"], w["kv_a_norm"], w["w_uk_pairs"])


def _softmax_step(s, v, m_scr, l_scr, acc_scr):
    m_prev = m_scr[...]
    m_new = jnp.maximum(m_prev, jnp.max(s, axis=-1, keepdims=True))
    alpha = jnp.exp(m_prev - m_new)
    p = jnp.exp(s - m_new)
    l_scr[...] = alpha * l_scr[...] + jnp.sum(p, axis=-1, keepdims=True)
    acc_scr[...] = alpha * acc_scr[...] + jnp.dot(p.astype(BF16), v, preferred_element_type=F32)
    m_scr[...] = m_new


def _mla_prompt_body(qcat_ref, kcat_ref, wuv_ref, out_ref, m_scr, acc_scr, *, tq):
    qi = pl.program_id(1)
    m_scr[...] = jnp.full(m_scr.shape, NEG_INF, F32)
    acc_scr[...] = jnp.zeros(acc_scr.shape, F32)

    def chunk(kc, masked):
        k = kcat_ref[pl.ds(pl.multiple_of(kc * tq, tq), tq), :]
        for h in range(MLA_HEADS):
            s = _dot_nt(qcat_ref[:, h * QCAT:(h + 1) * QCAT], k)
            if masked:
                tok = lax.broadcasted_iota(jnp.int32, s.shape, 0)
                col = lax.broadcasted_iota(jnp.int32, s.shape, 1)
                s = jnp.where(col <= tok, s, NEG_INF)
            m_prev = m_scr[h]
            m_new = jnp.maximum(m_prev, jnp.max(s, axis=-1, keepdims=True))
            p = jnp.exp(s - m_new).astype(BF16)
            acc_scr[h] = jnp.exp(m_prev - m_new) * acc_scr[h] + jnp.dot(p, k, preferred_element_type=F32)
            m_scr[h] = m_new

    def past(kc, carry):
        chunk(kc, False)
        return carry

    lax.fori_loop(0, qi, past, 0)
    chunk(qi, True)
    y = jnp.zeros(out_ref.shape, F32)
    for h in range(MLA_HEADS):
        acc = acc_scr[h]
        o = (acc[:, :KV_LORA] / acc[:, QCAT - 1:QCAT]).astype(BF16)
        y = y + jnp.dot(o, wuv_ref[h], preferred_element_type=F32)
    out_ref[...] = y.astype(out_ref.dtype)


def _mla_prompt(qcat, kcat, wuv_pad, batch, seq, tq):
    nq = seq // tq
    return pl.pallas_call(
        functools.partial(_mla_prompt_body, tq=tq),
        grid=(batch, nq),
        in_specs=[pl.BlockSpec((tq, MLA_HEADS * QCAT), lambda b, i: (b * nq + i, 0)),
                  pl.BlockSpec((seq, QCAT), lambda b, i: (b, 0)),
                  _full_spec((MLA_HEADS, KV_LORA, MLA_HEADS * MLA_V))],
        out_specs=pl.BlockSpec((tq, MLA_HEADS * MLA_V), lambda b, i: (b * nq + i, 0)),
        out_shape=jax.ShapeDtypeStruct((batch * seq, MLA_HEADS * MLA_V), BF16),
        scratch_shapes=[pltpu.VMEM((MLA_HEADS, tq, 1), F32), pltpu.VMEM((MLA_HEADS, tq, QCAT), F32)],
        compiler_params=_cparams("arbitrary", "arbitrary"),
        name="mla_prompt",
    )(qcat, kcat, wuv_pad)


def _alibi_slope(h):
    return 2.0 ** (-8.0 * (h + 1) / MOBA_HEADS)


def _block_choice(scores, n_valid):
    lane = lax.broadcasted_iota(jnp.int32, scores.shape, 1)
    n = lane & 7
    s = jnp.where(n < n_valid, scores, NEG_INF)
    beaten = jnp.zeros(scores.shape, F32)
    for d in range(1, 8):
        other = jnp.where(n >= d, pltpu.roll(s, d, axis=1), pltpu.roll(s, 128 - 8 + d, axis=1))
        wins = (other > s) | ((other == s) & (n >= d))
        beaten = beaten + wins.astype(F32)
    return jnp.where((beaten < MOBA_TOPK) & (n < n_valid), 1.0, 0.0)


def _moba_prompt_body(qf_ref, kf_ref, qp_ref, kp_ref, vp_ref, out_ref, kmean_scr, sel_scr, m_scr, acc_scr):
    qi = pl.program_id(1)
    j = pl.program_id(2)
    nblk = kmean_scr.shape[0]
    blk = MOBA_BLOCK

    @pl.when((qi == 0) & (j == 0))
    def _():
        kmean_scr[...] = jnp.zeros(kmean_scr.shape, F32)

    @pl.when(j == 0)
    def _():
        m_scr[...] = jnp.full(m_scr.shape, NEG_INF, F32)
        acc_scr[...] = jnp.zeros(acc_scr.shape, F32)
        km = kmean_scr[...]
        tiled = jnp.concatenate([km] * MOBA_HEADS + [jnp.zeros((128 - MOBA_HEADS * nblk, MOBA_WIDTH), F32)], axis=0)
        row_h = lax.broadcasted_iota(jnp.int32, tiled.shape, 0) // nblk
        lane_h = lax.broadcasted_iota(jnp.int32, tiled.shape, 1) // MOBA_DH
        kmbd = jnp.where(row_h == lane_h, tiled, 0.0)
        scores = lax.dot_general(qf_ref[...], kmbd, (((1,), (1,)), ((), ())),
                                 preferred_element_type=F32, precision=lax.Precision.HIGHEST)
        sel_scr[...] = _block_choice(scores, qi)

    @pl.when(j <= qi)
    def _():
        diag = (j == qi).astype(F32)
        r = lax.broadcasted_iota(jnp.int32, (blk, blk), 0)
        c = lax.broadcasted_iota(jnp.int32, (blk, blk), 1)
        dist = ((qi - j) * blk + (r - c)).astype(F32)
        causal = jnp.where(r >= c, diag, 0.0)
        er = lax.broadcasted_iota(jnp.int32, (128, MOBA_PADW), 0)
        ec = lax.broadcasted_iota(jnp.int32, (128, MOBA_PADW), 1)
        spread = jnp.where(((er >> 3) == (ec >> 7)) & ((er & 7) == j), 1.0 - diag, 0.0).astype(BF16)
        chosen = jnp.dot(sel_scr[...].astype(BF16), spread, preferred_element_type=F32)
        for h in range(MOBA_HEADS):
            sl = slice(h * MOBA_PAD, (h + 1) * MOBA_PAD)
            s = _dot_nt(qp_ref[:, sl], kp_ref[:, sl])
            s = s - _alibi_slope(h) * dist
            allowed = causal + jnp.concatenate([chosen[:, sl]] * (blk // MOBA_PAD), axis=1)
            s = jnp.where(allowed > 0.5, s, NEG_INF)
            m_prev = m_scr[h]
            m_new = jnp.maximum(m_prev, jnp.max(s, axis=-1, keepdims=True))
            p = jnp.exp(s - m_new).astype(BF16)
            acc_scr[h] = jnp.exp(m_prev - m_new) * acc_scr[h] + jnp.dot(p, vp_ref[:, sl], preferred_element_type=F32)
            m_scr[h] = m_new

    @pl.when(j == qi)
    def _():
        kmean_scr[pl.ds(qi, 1), :] = jnp.mean(kf_ref[...], axis=0, keepdims=True)
        head_lane = lax.broadcasted_iota(jnp.int32, (blk, MOBA_PAD), 1) < MOBA_DH
        for h in range(MOBA_HEADS):
            acc = acc_scr[h]
            o = jnp.where(head_lane, acc / acc[:, MOBA_PAD - 1:MOBA_PAD], 0.0)
            out_ref[:, h * MOBA_PAD:(h + 1) * MOBA_PAD] = o.astype(out_ref.dtype)


def _moba_prompt(qm, km, qp, kp, vp, batch, seq):
    nb = seq // MOBA_BLOCK
    assert nb <= 8 and seq % MOBA_BLOCK == 0
    qmap = lambda b, i, j: (b * nb + i, 0)
    kmap = lambda b, i, j: (b * nb + jnp.minimum(i, j), 0)
    blk = MOBA_BLOCK
    return pl.pallas_call(
        _moba_prompt_body,
        grid=(batch, nb, nb),
        in_specs=[pl.BlockSpec((blk, MOBA_WIDTH), qmap), pl.BlockSpec((blk, MOBA_WIDTH), qmap),
                  pl.BlockSpec((blk, MOBA_PADW), qmap),
                  pl.BlockSpec((blk, MOBA_PADW), kmap), pl.BlockSpec((blk, MOBA_PADW), kmap)],
        out_specs=pl.BlockSpec((blk, MOBA_PADW), qmap),
        out_shape=jax.ShapeDtypeStruct((batch * seq, MOBA_PADW), BF16),
        scratch_shapes=[pltpu.VMEM((8, MOBA_WIDTH), F32), pltpu.VMEM((blk, 128), F32),
                        pltpu.VMEM((MOBA_HEADS, blk, 1), F32), pltpu.VMEM((MOBA_HEADS, blk, MOBA_PAD), F32)],
        compiler_params=_cparams("arbitrary", "arbitrary", "arbitrary"),
        name="moba_prompt",
    )(qm, km, qp, kp, vp)


def _sigmoid(g):
    return 1.0 / (1.0 + jnp.exp(-g))


def _merge_body(x_ref, ya_ref, ob_ref, an_ref, wg_ref, wua_ref, wub_ref, wo_ref, h_ref):
    x = x_ref[...]
    xn = _rms(x, an_ref[...]).astype(BF16)
    g = jnp.dot(xn, wg_ref[...], preferred_element_type=F32)
    y_a = jnp.dot(ya_ref[...], wua_ref[...], preferred_element_type=F32)
    y_b = jnp.dot(ob_ref[...], wub_ref[...], preferred_element_type=F32)
    mix = _sigmoid(g[:, :D_MODEL]) * y_a + _sigmoid(g[:, D_MODEL:]) * y_b
    h_ref[...] = x + jnp.dot(mix.astype(BF16), wo_ref[...], preferred_element_type=F32)


def _merge(x2, ya, ob, w_up_b, w, tm):
    n = x2.shape[0]
    wb = ob.shape[1]
    row = lambda i: (i, 0)
    return pl.pallas_call(
        _merge_body,
        grid=(n // tm,),
        in_specs=[pl.BlockSpec((tm, D_MODEL), row), pl.BlockSpec((tm, MLA_HEADS * MLA_V), row),
                  pl.BlockSpec((tm, wb), row),
                  _full_spec((1, D_MODEL)), _full_spec((D_MODEL, 2 * D_MODEL)),
                  _full_spec((MLA_HEADS * MLA_V, D_MODEL)), _full_spec((wb, D_MODEL)),
                  _full_spec((D_MODEL, D_MODEL))],
        out_specs=pl.BlockSpec((tm, D_MODEL), row),
        out_shape=jax.ShapeDtypeStruct((n, D_MODEL), F32),
        compiler_params=_cparams("arbitrary"),
        name="merge",
    )(x2, ya, ob, w["attn_norm"], w["w_g"], w["w_up_a"], w_up_b, w["w_o"])


def _topk_ranks(s, key):
    rank = jnp.full(s.shape, float(PEER_TOPK), F32)
    work = s
    vals = []
    for r in range(PEER_TOPK):
        m = jnp.max(work, axis=0, keepdims=True)
        kmin = jnp.min(jnp.where(work == m, key, 1e9), axis=0, keepdims=True)
        hit = key == kmin
        rank = jnp.where(hit, float(r), rank)
        work = jnp.where(hit, -jnp.inf, work)
        vals.append(m)
    return rank, vals


def _gelu(x):
    return 0.5 * x * (1.0 + lax.erf(x * np.float32(np.sqrt(0.5))))


def _peer_body(h_ref, fn_ref, wq_ref, k1_ref, k2_ref, u_ref, vt_ref, on_ref, y_ref,
               xn_scr, qt_scr, r2_scr, e2_scr, nb_scr, p1_scr, acc_scr):
    c = pl.program_id(1)
    t = h_ref.shape[0]
    kk = PEER_TOPK

    @pl.when(c == 0)
    def _():
        xn = _rms(h_ref[...], fn_ref[...]).astype(BF16)
        xn_scr[...] = xn
        qt_scr[...] = _dot_nt(wq_ref[...], xn)
        acc_scr[...] = jnp.zeros(acc_scr.shape, F32)
        key128 = lax.broadcasted_iota(jnp.int32, (PEER_NKEYS, t), 0).astype(F32)
        r16 = lax.broadcasted_iota(jnp.int32, (kk, t), 0).astype(F32)
        r8 = lax.broadcasted_iota(jnp.int32, (8, t), 0).astype(F32)

        def per_head(hd, carry):
            base = pl.multiple_of(hd * PEER_DKEY, PEER_DKEY)
            q1 = qt_scr[pl.ds(base, PEER_DKEY // 2), :].astype(BF16)
            q2 = qt_scr[pl.ds(base + PEER_DKEY // 2, PEER_DKEY // 2), :].astype(BF16)
            s1 = jnp.dot(k1_ref[hd], q1, preferred_element_type=F32)
            s2 = jnp.dot(k2_ref[hd], q2, preferred_element_type=F32)
            rank1, v1r = _topk_ranks(s1, key128)
            rank2, v2r = _topk_ranks(s2, key128)
            v1, v1h = jnp.concatenate(v1r, axis=0), jnp.concatenate(v1r[:8], axis=0)
            v2, v2h = jnp.concatenate(v2r, axis=0), jnp.concatenate(v2r[:8], axis=0)
            slabs = [(v1 + v2r[0], r16 * kk, None),
                     (v1h + v2r[1], r8 * kk + 1.0, None),
                     (v2 + v1r[0], r16, r16 >= 2.0),
                     (v2h + v1r[1], r8 + float(kk), r8 >= 2.0)]
            for b in (2, 3, 4):
                slabs.append((v1h + v2r[b], r8 * kk + float(b), (r8 >= 2.0) & ((r8 + 1.0) * (b + 1) <= kk)))
            cand = jnp.concatenate([vals if ok is None else jnp.where(ok, vals, -jnp.inf) for vals, _, ok in slabs],
                                   axis=0)
            ckey = jnp.concatenate([key if ok is None else jnp.where(ok, key, 1e6) for _, key, ok in slabs], axis=0)
            rankc, _ = _topk_ranks(cand, ckey)
            selc = jnp.where(rankc < float(kk), 1.0, 0.0)
            sel_a, sel_b, sel_c, sel_d = selc[0:16], selc[16:24], selc[24:40], selc[40:48]
            sel_e = [selc[48:56], selc[56:64], selc[64:72]]
            e1, e1h = jnp.exp(v1 - v1r[0]), jnp.exp(v1h - v1r[0])
            e2, e2h = jnp.exp(v2 - v2r[0]), jnp.exp(v2h - v2r[0])
            e1r = [jnp.exp(v1r[a] - v1r[0]) for a in range(kk)]
            e2r = [jnp.exp(v2r[b] - v2r[0]) for b in range(kk)]
            colsum = lambda x: jnp.sum(x, axis=0, keepdims=True)
            zero8 = jnp.zeros((8, t), F32)
            nb = (sel_a + jnp.concatenate([sel_b + sel_e[0] + sel_e[1] + sel_e[2], zero8], axis=0)
                  + jnp.where(r16 == 0.0, colsum(sel_c), 0.0) + jnp.where(r16 == 1.0, colsum(sel_d), 0.0))
            zsum = (colsum(sel_a * e1) * e2r[0] + colsum(sel_b * e1h) * e2r[1]
                    + colsum(sel_c * e2) * e1r[0] + colsum(sel_d * e2h) * e1r[1])
            for n, b in enumerate((2, 3, 4)):
                zsum = zsum + colsum(sel_e[n] * e1h) * e2r[b]
            e2r = [e / zsum for e in e2r]
            nbx = jnp.zeros((PEER_NKEYS, t), F32)
            p1x = jnp.zeros((PEER_NKEYS, t), F32)
            e2x = jnp.zeros((PEER_NKEYS, t), F32)
            for a in range(kk):
                hit1 = rank1 == float(a)
                nbx = jnp.where(hit1, nb[a:a + 1], nbx)
                p1x = jnp.where(hit1, e1r[a], p1x)
                e2x = jnp.where(rank2 == float(a), e2r[a], e2x)
            r2_scr[hd] = rank2.astype(BF16)
            e2_scr[hd] = e2x.astype(BF16)
            nb_scr[hd] = nbx
            p1_scr[hd] = p1x
            return carry

        lax.fori_loop(0, PEER_HEADS, per_head, 0)

    xn = xn_scr[...]
    acc = None
    for kt in range(PEER_ECHUNK // PEER_KTILE):
        rows = slice(kt * PEER_KTILE, (kt + 1) * PEER_KTILE)
        act = _dot_nt(u_ref[rows, :], xn)
        parts = []
        for ii in range(PEER_KTILE // PEER_NKEYS):
            i = c * PEER_ICHUNK + kt * (PEER_KTILE // PEER_NKEYS) + ii
            wgt = jnp.zeros((PEER_NKEYS, t), BF16)
            for hd in range(PEER_HEADS):
                nbrow = nb_scr[hd, pl.ds(i, 1), :].astype(BF16)
                p1row = p1_scr[hd, pl.ds(i, 1), :].astype(BF16)
                wgt = wgt + jnp.where(r2_scr[hd] < nbrow, e2_scr[hd] * p1row, jnp.zeros((), BF16))
            parts.append(wgt * _gelu(act[ii * PEER_NKEYS:(ii + 1) * PEER_NKEYS]).astype(BF16))
        piece = jnp.dot(vt_ref[:, rows], jnp.concatenate(parts, axis=0), preferred_element_type=F32)
        acc = piece if acc is None else acc + piece
    acc_scr[...] += acc

    @pl.when(c == pl.num_programs(1) - 1)
    def _():
        h2 = h_ref[...] + acc_scr[...].T
        y_ref[...] = _rms(h2, on_ref[...])


def _peer(h2d, w, t):
    n = h2d.shape[0]
    n_exp = w["peer_u"].shape[0]
    nchunk = n_exp // PEER_ECHUNK
    slab = (PEER_HEADS, PEER_NKEYS, t)
    return pl.pallas_call(
        _peer_body,
        grid=(n // t, nchunk),
        in_specs=[pl.BlockSpec((t, D_MODEL), lambda i, c: (i, 0)),
                  _full_spec((1, D_MODEL)), _full_spec((PEER_HEADS * PEER_DKEY, D_MODEL)),
                  _full_spec((PEER_HEADS, PEER_NKEYS, PEER_DKEY // 2)),
                  _full_spec((PEER_HEADS, PEER_NKEYS, PEER_DKEY // 2)),
                  pl.BlockSpec((PEER_ECHUNK, D_MODEL), lambda i, c: (c, 0)),
                  pl.BlockSpec((D_MODEL, PEER_ECHUNK), lambda i, c: (0, c)),
                  _full_spec((1, D_MODEL))],
        out_specs=pl.BlockSpec((t, D_MODEL), lambda i, c: (i, 0)),
        out_shape=jax.ShapeDtypeStruct((n, D_MODEL), F32),
        scratch_shapes=[pltpu.VMEM((t, D_MODEL), BF16), pltpu.VMEM((PEER_HEADS * PEER_DKEY, t), F32),
                        pltpu.VMEM(slab, BF16), pltpu.VMEM(slab, BF16), pltpu.VMEM(slab, F32), pltpu.VMEM(slab, F32),
                        pltpu.VMEM((D_MODEL, t), F32)],
        compiler_params=_cparams("arbitrary", "arbitrary"),
        name="peer",
    )(h2d, w["ffn_norm"], w["peer_wq_t"], w["peer_k1"], w["peer_k2"], w["peer_u"], w["peer_vt"], w["final_norm"])


SAMPLE_PAGES_PER_STEP = 16


def _mla_sample_body(pt_ref, q_ref, knew_ref, *rest, pages, dec):
    c_refs = rest[:pages]
    r_refs = rest[pages:2 * pages]
    wuv_ref, out_ref, q_scr, knew_scr, m_scr, l_scr, acc_scr = rest[2 * pages:]
    s_id = pl.program_id(1)
    rows = MLA_HEADS * dec

    @pl.when(s_id == 0)
    def _():
        for h in range(MLA_HEADS):
            q_scr[h * dec:(h + 1) * dec, :] = q_ref[0, :, h * QCAT:(h + 1) * QCAT].astype(F32)
        m_scr[...] = jnp.full(m_scr.shape, NEG_INF, F32)
        l_scr[...] = jnp.zeros(l_scr.shape, F32)
        acc_scr[...] = jnp.zeros(acc_scr.shape, F32)

    q = q_scr[...].astype(BF16)
    q_lat = q[:, :KV_LORA]
    q_rope = q[:, KV_LORA:KV_LORA + MLA_ROPE]
    cs = [c_refs[i][0].astype(BF16) for i in range(pages)]
    s = jnp.concatenate([_dot_nt(q_lat, cs[i])
                         + jnp.dot(q_rope, r_refs[i][0].astype(BF16), preferred_element_type=F32)
                         for i in range(pages)], axis=1)
    m_prev = m_scr[...]
    m_new = jnp.maximum(m_prev, jnp.max(s, axis=-1, keepdims=True))
    alpha = jnp.exp(m_prev - m_new)
    p = jnp.exp(s - m_new)
    l_scr[...] = alpha * l_scr[...] + jnp.sum(p, axis=-1, keepdims=True)
    p = p.astype(BF16)
    pv = jnp.zeros(acc_scr.shape, F32)
    for i in range(pages):
        pv = pv + jnp.dot(p[:, i * PAGE_SIZE:(i + 1) * PAGE_SIZE], cs[i], preferred_element_type=F32)
    acc_scr[...] = alpha * acc_scr[...] + pv
    m_scr[...] = m_new

    @pl.when(s_id == pl.num_programs(1) - 1)
    def _():
        knew_scr[...] = jnp.zeros(knew_scr.shape, F32)
        knew_scr[0:dec, :] = knew_ref[0].astype(F32)
        k = knew_scr[...].astype(BF16)
        s_new = _dot_nt(q, k)
        tok = lax.broadcasted_iota(jnp.int32, s_new.shape, 0) & (dec - 1)
        col = lax.broadcasted_iota(jnp.int32, s_new.shape, 1)
        s_new = jnp.where(col <= tok, s_new, NEG_INF)
        _softmax_step(s_new, k[:, :KV_LORA], m_scr, l_scr, acc_scr)
        o = (acc_scr[...] / l_scr[...]).astype(BF16)
        y = jnp.zeros((dec, MLA_HEADS * MLA_V), F32)
        for h in range(MLA_HEADS):
            y = y + jnp.dot(o[h * dec:(h + 1) * dec], wuv_ref[h], preferred_element_type=F32)
        out_ref[0] = y.astype(out_ref.dtype)


def _mla_sample(page_flat, qcat3, kcat3, cache_c, cache_r, wuv_pad, n_pages):
    bd, dec, _ = qcat3.shape
    assert dec & (dec - 1) == 0 and dec <= PAGE_SIZE
    pages = min(SAMPLE_PAGES_PER_STEP, n_pages)
    steps = n_pages // pages
    rows = MLA_HEADS * dec

    def page_map(i):
        return lambda b, s, pt: (pt[b * n_pages + s * pages + i], 0, 0)

    per_b = lambda b, s, pt: (b, 0, 0)
    grid_spec = pltpu.PrefetchScalarGridSpec(
        num_scalar_prefetch=1,
        grid=(bd, steps),
        in_specs=([pl.BlockSpec((1, dec, MLA_HEADS * QCAT), per_b), pl.BlockSpec((1, dec, QCAT), per_b)]
                  + [pl.BlockSpec((1, PAGE_SIZE, KV_LORA), page_map(i)) for i in range(pages)]
                  + [pl.BlockSpec((1, MLA_ROPE, PAGE_SIZE), page_map(i)) for i in range(pages)]
                  + [pl.BlockSpec((MLA_HEADS, KV_LORA, MLA_HEADS * MLA_V), lambda b, s, pt: (0, 0, 0))]),
        out_specs=pl.BlockSpec((1, dec, MLA_HEADS * MLA_V), per_b),
        scratch_shapes=[pltpu.VMEM((rows, QCAT), F32), pltpu.VMEM((PAGE_SIZE, QCAT), F32),
                        pltpu.VMEM((rows, 1), F32), pltpu.VMEM((rows, 1), F32), pltpu.VMEM((rows, KV_LORA), F32)],
    )
    return pl.pallas_call(
        functools.partial(_mla_sample_body, pages=pages, dec=dec),
        grid_spec=grid_spec,
        out_shape=jax.ShapeDtypeStruct((bd, dec, MLA_HEADS * MLA_V), BF16),
        compiler_params=_cparams("arbitrary", "arbitrary"),
        name="mla_sample",
    )(page_flat, qcat3, kcat3, *([cache_c] * pages), *([cache_r] * pages), wuv_pad)


def _group_choice(scores, group, n_valid, topk):
    lane = lax.broadcasted_iota(jnp.int32, scores.shape, 1)
    s = jnp.where(lane < n_valid, scores, NEG_INF)
    beaten = jnp.zeros(scores.shape, F32)
    for d in range(1, group):
        other = jnp.where(lane >= d, pltpu.roll(s, d, axis=1), pltpu.roll(s, 128 - group + d, axis=1))
        wins = (other > s) | ((other == s) & (lane >= d))
        beaten = beaten + wins.astype(F32)
    return jnp.where((beaten < topk) & (lane < n_valid), 1.0, 0.0)


def _moba_sample_body(pt_ref, qm_ref, kn_ref, vn_ref, e_ref, *rest, pages, dec, n_pages):
    k_refs = rest[:pages]
    v_refs = rest[pages:2 * pages]
    out_ref, qbd_scr, kmt_scr, s_scr, snew_scr, new_scr, l_scr, acc_scr = rest[2 * pages:]
    s_id = pl.program_id(1)
    n_k = n_pages // pages
    rows = MOBA_HEADS * dec
    n_blocks = n_pages // 2
    past = n_pages * PAGE_SIZE
    row = lax.broadcasted_iota(jnp.int32, (rows, PAGE_SIZE), 0)
    col = lax.broadcasted_iota(jnp.int32, (rows, PAGE_SIZE), 1)
    tok = row & (dec - 1)
    slope = jnp.zeros((rows, PAGE_SIZE), F32)
    for h in range(MOBA_HEADS):
        slope = jnp.where(row // dec == h, _alibi_slope(h), slope)

    @pl.when(s_id == 0)
    def _():
        tiled = jnp.concatenate([qm_ref[0]] * MOBA_HEADS, axis=0)
        row_h = lax.broadcasted_iota(jnp.int32, tiled.shape, 0) // dec
        lane_h = lax.broadcasted_iota(jnp.int32, tiled.shape, 1) // MOBA_DH
        qbd_scr[...] = jnp.where(row_h == lane_h, tiled * MOBA_SCALE, 0.0)
        kmt_scr[...] = jnp.zeros(kmt_scr.shape, F32)

    @pl.when(s_id < n_k)
    def _():
        qbd = qbd_scr[...].astype(BF16)
        lane = lax.broadcasted_iota(jnp.int32, kmt_scr.shape, 1)
        for i in range(0, pages, 2):
            kt0 = k_refs[i][0]
            kt1 = k_refs[i + 1][0]
            page = s_id * pages + i
            s_scr[page] = jnp.dot(qbd, kt0.astype(BF16), preferred_element_type=F32)
            s_scr[page + 1] = jnp.dot(qbd, kt1.astype(BF16), preferred_element_type=F32)
            mean = 0.5 * (jnp.mean(kt0, axis=1, keepdims=True) + jnp.mean(kt1, axis=1, keepdims=True))
            kmt_scr[...] = jnp.where(lane == page // 2, mean, kmt_scr[...])

    @pl.when(s_id == n_k)
    def _():
        scores = jnp.dot(qbd_scr[...], kmt_scr[...], preferred_element_type=F32, precision=lax.Precision.HIGHEST)
        sel = _group_choice(scores, max(n_blocks, 1), n_blocks, MOBA_TOPK)
        chosen = jnp.dot(sel.astype(BF16), e_ref[...], preferred_element_type=F32)
        new_scr[...] = jnp.zeros(new_scr.shape, F32)
        new_scr[0:dec, :] = kn_ref[0]
        s_new = _dot_nt(qbd_scr[...].astype(BF16), new_scr[...].astype(BF16))
        s_new = s_new - slope * (tok - col).astype(F32)
        s_new = jnp.where(col <= tok, s_new, NEG_INF)
        bias = slope * (past + tok - col).astype(F32)
        mx = s_new
        for page in range(n_pages):
            cols = slice(page * PAGE_SIZE, (page + 1) * PAGE_SIZE)
            sm = jnp.where(chosen[:, cols] > 0.5, s_scr[page] - (bias - slope * float(page * PAGE_SIZE)), NEG_INF)
            s_scr[page] = sm
            mx = jnp.maximum(mx, sm)
        m = jnp.max(mx, axis=-1, keepdims=True)
        p_new = jnp.exp(s_new - m)
        snew_scr[...] = p_new
        lsum = p_new
        for page in range(n_pages):
            p = jnp.exp(s_scr[page] - m)
            s_scr[page] = p
            lsum = lsum + p
        l_scr[...] = jnp.sum(lsum, axis=-1, keepdims=True)
        acc_scr[...] = jnp.zeros(acc_scr.shape, F32)

    @pl.when(s_id >= n_k)
    def _():
        pv = jnp.zeros(acc_scr.shape, F32)
        for i in range(pages):
            page = (s_id - n_k) * pages + i
            pv = pv + _dot_nt(s_scr[page].astype(BF16), v_refs[i][0].astype(BF16))
        acc_scr[...] += pv

    @pl.when(s_id == 2 * n_k - 1)
    def _():
        new_scr[...] = jnp.zeros(new_scr.shape, F32)
        new_scr[0:dec, :] = vn_ref[0]
        acc = acc_scr[...] + jnp.dot(snew_scr[...].astype(BF16), new_scr[...].astype(BF16),
                                     preferred_element_type=F32)
        o = acc / l_scr[...]
        row_h = lax.broadcasted_iota(jnp.int32, o.shape, 0) // dec
        lane_h = lax.broadcasted_iota(jnp.int32, o.shape, 1) // MOBA_DH
        o = jnp.where(row_h == lane_h, o, 0.0)
        y = o[0:dec]
        for h in range(1, MOBA_HEADS):
            y = y + o[h * dec:(h + 1) * dec]
        out_ref[0] = y.astype(out_ref.dtype)


def _moba_sample(page_flat, qm3, km3, vm3, cache_kt, cache_vt, n_pages):
    bd, dec, _ = qm3.shape
    assert dec == 8, "row groups of the stacked heads must be one sublane tile"
    assert n_pages % 2 == 0, "the past must be a whole number of MoBA blocks"
    n_blocks = n_pages // 2
    assert n_blocks & (n_blocks - 1) == 0 and n_blocks <= 128
    pages = min(SAMPLE_PAGES_PER_STEP, n_pages)
    n_k = n_pages // pages
    rows = MOBA_HEADS * dec

    def k_map(i):
        return lambda b, s, pt: (pt[b * n_pages + jnp.minimum(s, n_k - 1) * pages + i], 0, 0)

    def v_map(i):
        return lambda b, s, pt: (pt[b * n_pages + jnp.maximum(s - n_k, 0) * pages + i], 0, 0)

    per_b = lambda b, s, pt: (b, 0, 0)
    n_keys = n_pages * PAGE_SIZE
    member = (jnp.arange(n_keys)[None, :] // MOBA_BLOCK == jnp.arange(128)[:, None]).astype(BF16)
    grid_spec = pltpu.PrefetchScalarGridSpec(
        num_scalar_prefetch=1,
        grid=(bd, 2 * n_k),
        in_specs=([pl.BlockSpec((1, dec, MOBA_WIDTH), per_b)] * 3
                  + [pl.BlockSpec((128, n_keys), lambda b, s, pt: (0, 0))]
                  + [pl.BlockSpec((1, MOBA_WIDTH, PAGE_SIZE), k_map(i)) for i in range(pages)]
                  + [pl.BlockSpec((1, MOBA_WIDTH, PAGE_SIZE), v_map(i)) for i in range(pages)]),
        out_specs=pl.BlockSpec((1, dec, MOBA_WIDTH), per_b),
        scratch_shapes=[pltpu.VMEM((rows, MOBA_WIDTH), F32), pltpu.VMEM((MOBA_WIDTH, 128), F32),
                        pltpu.VMEM((n_pages, rows, PAGE_SIZE), F32), pltpu.VMEM((rows, PAGE_SIZE), F32),
                        pltpu.VMEM((PAGE_SIZE, MOBA_WIDTH), F32),
                        pltpu.VMEM((rows, 1), F32), pltpu.VMEM((rows, MOBA_WIDTH), F32)],
    )
    return pl.pallas_call(
        functools.partial(_moba_sample_body, pages=pages, dec=dec, n_pages=n_pages),
        grid_spec=grid_spec,
        out_shape=jax.ShapeDtypeStruct((bd, dec, MOBA_WIDTH), BF16),
        compiler_params=_cparams("arbitrary", "arbitrary"),
        name="moba_sample",
    )(page_flat, qm3, km3, vm3, member, *([cache_kt] * pages), *([cache_vt] * pages))


def _rope_tables(pos):
    half = MLA_ROPE // 2
    inv = ROPE_THETA ** (-jnp.arange(half, dtype=F32) / half)
    ang = pos.astype(F32)[:, None] * inv
    cos, sin = jnp.cos(ang), jnp.sin(ang)
    pad = jnp.zeros((pos.shape[0], ROPE_PAD - MLA_ROPE), F32)
    return jnp.concatenate([cos, cos, pad], axis=1), jnp.concatenate([-sin, sin, pad], axis=1)


def _swap_halves(w):
    half = MLA_ROPE // 2
    return jnp.concatenate([w[..., half:], w[..., :half]], axis=-1)


def _layer_weights(l, attn_norm, w_in, q_a_norm, w_q_b, kv_a_norm, w_uk, w_uv, w_up_a, w_up_b, w_o,
                   ffn_norm, peer_w_query, peer_keys_1, peer_keys_2, peer_u, peer_v, final_norm):
    sizes = (Q_LORA, KV_LORA, MLA_ROPE, MOBA_WIDTH, MOBA_WIDTH, MOBA_WIDTH, D_MODEL, D_MODEL)
    qa, ckv, kr, qm, km, vm, ga, gb = jnp.split(w_in[l], [int(v) for v in np.cumsum(sizes)[:-1]], axis=1)
    pad = jnp.zeros((D_MODEL, ROPE_PAD - MLA_ROPE), F32)
    w_a = jnp.concatenate([qa, ckv, qm, km, vm, kr, pad, _swap_halves(kr), pad], axis=1)
    wq = w_q_b[l].reshape(Q_LORA, MLA_HEADS, MLA_NOPE + MLA_ROPE)
    nope = wq[:, :, :MLA_NOPE].reshape(Q_LORA, MLA_HEADS * MLA_NOPE)
    rope = wq[:, :, MLA_NOPE:]
    padq = ((0, 0), (0, 0), (0, ROPE_PAD - MLA_ROPE))
    rope_p = jnp.pad(rope, padq).reshape(Q_LORA, MLA_HEADS * ROPE_PAD)
    ropes_p = jnp.pad(_swap_halves(rope), padq).reshape(Q_LORA, MLA_HEADS * ROPE_PAD)
    w_qb = jnp.concatenate([nope, rope_p, ropes_p], axis=1)
    wuk_t = jnp.transpose(w_uk[l], (1, 2, 0))
    z = jnp.zeros((MLA_NOPE, KV_LORA), F32)
    pairs = [jnp.concatenate([jnp.concatenate([wuk_t[2 * p], z], axis=1),
                              jnp.concatenate([z, wuk_t[2 * p + 1]], axis=1)], axis=0)
             for p in range(MLA_HEADS // 2)]
    wuv_t = jnp.transpose(w_uv[l], (1, 0, 2))
    wuv_pad = jnp.stack([jnp.pad(wuv_t[h], ((0, 0), (h * MLA_V, (MLA_HEADS - 1 - h) * MLA_V)))
                         for h in range(MLA_HEADS)])
    wub = w_up_b[l].reshape(MOBA_HEADS, MOBA_DH, D_MODEL)
    wub_pad = jnp.pad(wub, ((0, 0), (0, MOBA_PAD - MOBA_DH), (0, 0))).reshape(MOBA_PADW, D_MODEL)
    return {
        "attn_norm": attn_norm[l][None, :], "w_a": w_a.astype(BF16),
        "q_a_norm": q_a_norm[l][None, :], "w_qb": w_qb.astype(BF16),
        "kv_a_norm": kv_a_norm[l][None, :], "w_uk_pairs": jnp.stack(pairs).astype(BF16),
        "wuv_pad": wuv_pad.astype(BF16),
        "w_g": jnp.concatenate([ga, gb], axis=1).astype(BF16),
        "w_up_a": w_up_a[l].astype(BF16), "w_up_b": w_up_b[l].astype(BF16), "w_up_b_pad": wub_pad.astype(BF16),
        "w_o": w_o[l].astype(BF16),
        "ffn_norm": ffn_norm[l][None, :], "peer_wq_t": peer_w_query[l].T.astype(BF16),
        "peer_k1": peer_keys_1[l].astype(BF16), "peer_k2": peer_keys_2[l].astype(BF16),
        "peer_u": peer_u[l].astype(BF16), "peer_vt": peer_v[l].T.astype(BF16),
        "final_norm": final_norm[None, :],
    }


def _token_tile(n, want):
    t = min(want, n)
    assert n % t == 0
    return t


def kernel(x_prompt, x_sample, cache_mla_latent, cache_mla_rope, cache_moba_k, cache_moba_v, page_table,
           attn_norm, w_in, q_a_norm, w_q_b, kv_a_norm, w_uk, w_uv, w_up_a, w_up_b, w_o,
           ffn_norm, peer_w_query, peer_keys_1, peer_keys_2, peer_u, peer_v, final_norm):
    batch, seq, _ = x_prompt.shape
    bd, dec, _ = x_sample.shape
    depth = w_in.shape[0]
    n_pages = page_table.shape[1]
    past = n_pages * PAGE_SIZE
    n_p, n_s = batch * seq, bd * dec
    tm_p = _token_tile(seq, 256)
    tm_s = _token_tile(n_s, 256)
    assert tm_s % dec == 0
    cs_p, sn_p = _rope_tables(jnp.arange(seq))
    cs_s, sn_s = _rope_tables(past + jnp.arange(dec))
    cs_s, sn_s = jnp.tile(cs_s, (tm_s // dec, 1)), jnp.tile(sn_s, (tm_s // dec, 1))
    page_flat = page_table.reshape(-1).astype(jnp.int32)

    assert depth == 1, "the PEER kernel applies the final norm, so it closes the only layer"
    l = 0
    w = _layer_weights(l, attn_norm, w_in, q_a_norm, w_q_b, kv_a_norm, w_uk, w_uv, w_up_a, w_up_b, w_o,
                       ffn_norm, peer_w_query, peer_keys_1, peer_keys_2, peer_u, peer_v, final_norm)
    hp = x_prompt.reshape(n_p, D_MODEL)
    qcat, kcat, c, kr, qm, km, vm, qp, kp, vp = _inproj(hp, cs_p, sn_p, w, tm_p)
    ya = _mla_prompt(qcat, kcat, w["wuv_pad"], batch, seq, _token_tile(seq, 256))
    ob = _moba_prompt(qm, km, qp, kp, vp, batch, seq)
    y_p = _peer(_merge(hp, ya, ob, w["w_up_b_pad"], w, tm_p), w, _token_tile(n_p, 256))
    outs_p = (c.reshape(1, batch, seq, KV_LORA), kr.reshape(1, batch, seq, MLA_ROPE),
              km.reshape(1, batch, seq, MOBA_HEADS, MOBA_DH), vm.reshape(1, batch, seq, MOBA_HEADS, MOBA_DH))
    hs = x_sample.reshape(n_s, D_MODEL)
    qcat, kcat, c, kr, qm, km, vm, _, _, _ = _inproj(hs, cs_s, sn_s, w, tm_s)
    rope_t = jnp.swapaxes(cache_mla_rope[l], 1, 2)
    moba_kt = jnp.transpose(cache_moba_k[l], (0, 2, 3, 1)).reshape(-1, MOBA_WIDTH, PAGE_SIZE)
    moba_vt = jnp.transpose(cache_moba_v[l], (0, 2, 3, 1)).reshape(-1, MOBA_WIDTH, PAGE_SIZE)
    ya = _mla_sample(page_flat, qcat.reshape(bd, dec, -1), kcat.reshape(bd, dec, -1),
                     cache_mla_latent[l], rope_t, w["wuv_pad"], n_pages)
    ob = _moba_sample(page_flat, qm.reshape(bd, dec, -1), km.reshape(bd, dec, -1), vm.reshape(bd, dec, -1),
                      moba_kt, moba_vt, n_pages)
    y_s = _peer(_merge(hs, ya.reshape(n_s, -1), ob.reshape(n_s, -1), w["w_up_b"], w, tm_s), w,
                _token_tile(n_s, 256))
    outs_s = (c.reshape(1, bd, dec, KV_LORA), kr.reshape(1, bd, dec, MLA_ROPE),
              km.reshape(1, bd, dec, MOBA_HEADS, MOBA_DH), vm.reshape(1, bd, dec, MOBA_HEADS, MOBA_DH))
    return (y_p.reshape(batch, seq, D_MODEL), y_s.reshape(bd, dec, D_MODEL), *outs_p, *outs_s)
```

```python
import functools

import numpy as np
import jax
import jax.numpy as jnp
from jax import lax
from jax.experimental import pallas as pl
from jax.experimental.pallas import tpu as pltpu

F32 = jnp.float32
BF16 = jnp.bfloat16

D_MODEL = 1024
PAGE_SIZE = 128

MLA_HEADS = 8
MLA_NOPE = 64
MLA_ROPE = 32
MLA_V = 64
Q_LORA = 768
KV_LORA = 256
ROPE_THETA = 10000.0
MLA_SCALE = (MLA_NOPE + MLA_ROPE) ** -0.5
ROPE_PAD = 128
QCAT = KV_LORA + ROPE_PAD

MOBA_HEADS = 8
MOBA_DH = 64
MOBA_WIDTH = MOBA_HEADS * MOBA_DH
MOBA_BLOCK = 256
MOBA_TOPK = 3
MOBA_SCALE = MOBA_DH ** -0.5
MOBA_PAD = 128
MOBA_PADW = MOBA_HEADS * MOBA_PAD

PEER_HEADS = 8
PEER_NKEYS = 128
PEER_DKEY = 256
PEER_TOPK = 16
PEER_ECHUNK = 4096
PEER_ICHUNK = PEER_ECHUNK // PEER_NKEYS
PEER_KTILE = 256
PEER_NCAND = 72

RMS_EPS = 1e-6
NEG_INF = -1e30

VMEM_LIMIT_BYTES = 56 * 1024 * 1024


def _cparams(*sem):
    return pltpu.CompilerParams(dimension_semantics=sem, vmem_limit_bytes=VMEM_LIMIT_BYTES)


def _rms(xf, w):
    return xf * lax.rsqrt(jnp.mean(xf * xf, axis=-1, keepdims=True) + RMS_EPS) * w


def _dot_nt(a, b):
    return lax.dot_general(a, b, (((1,), (1,)), ((), ())), preferred_element_type=F32)


def _full_spec(shape):
    n = len(shape)
    return pl.BlockSpec(shape, lambda *_: (0,) * n)


_A_QA = 0
_A_CKV = Q_LORA
_A_QM = _A_CKV + KV_LORA
_A_KM = _A_QM + MOBA_WIDTH
_A_VM = _A_KM + MOBA_WIDTH
_A_KR = _A_VM + MOBA_WIDTH
_A_KRS = _A_KR + ROPE_PAD
_A_WIDTH = _A_KRS + ROPE_PAD
_Q_NOPE = 0
_Q_ROPE = MLA_HEADS * MLA_NOPE
_Q_ROPES = _Q_ROPE + MLA_HEADS * ROPE_PAD
_Q_WIDTH = _Q_ROPES + MLA_HEADS * ROPE_PAD


def _pad_heads(v, lane_lo, fill):
    out = []
    for p in range(MOBA_HEADS // 2):
        blk = v[:, p * 128:(p + 1) * 128]
        out.append(jnp.where(lane_lo, blk, fill))
        out.append(jnp.where(lane_lo, pltpu.roll(blk, MOBA_DH, axis=1), fill))
    return jnp.concatenate(out, axis=1)


def _inproj_body(x_ref, cs_ref, sn_ref, an_ref, wa_ref, qan_ref, wqb_ref, kvn_ref, wuk_ref,
                 qcat_ref, kcat_ref, c_ref, kr_ref, qm_ref, km_ref, vm_ref, qp_ref, kp_ref, vp_ref):
    xn = _rms(x_ref[...], an_ref[...]).astype(BF16)
    z = jnp.dot(xn, wa_ref[...], preferred_element_type=F32)
    cs = cs_ref[...]
    sn = sn_ref[...]
    qn = _rms(z[:, _A_QA:_A_QA + Q_LORA], qan_ref[...]).astype(BF16)
    q = jnp.dot(qn, wqb_ref[...], preferred_element_type=F32)
    for p in range(MLA_HEADS // 2):
        lat = jnp.dot(q[:, 128 * p:128 * (p + 1)].astype(BF16), wuk_ref[p], preferred_element_type=F32)
        for hl in range(2):
            h = 2 * p + hl
            qcat_ref[:, h * QCAT:h * QCAT + KV_LORA] = (lat[:, hl * KV_LORA:(hl + 1) * KV_LORA] * MLA_SCALE).astype(BF16)
            rope = (q[:, _Q_ROPE + h * ROPE_PAD:_Q_ROPE + (h + 1) * ROPE_PAD] * cs
                    + q[:, _Q_ROPES + h * ROPE_PAD:_Q_ROPES + (h + 1) * ROPE_PAD] * sn)
            qcat_ref[:, h * QCAT + KV_LORA:(h + 1) * QCAT] = (rope * MLA_SCALE).astype(BF16)
    c = _rms(z[:, _A_CKV:_A_CKV + KV_LORA], kvn_ref[...])
    c_ref[...] = c
    kr = z[:, _A_KR:_A_KR + ROPE_PAD] * cs + z[:, _A_KRS:_A_KRS + ROPE_PAD] * sn
    kr_ref[...] = kr[:, :MLA_ROPE]
    kcat_ref[:, :KV_LORA] = c.astype(BF16)
    ones_lane = lax.broadcasted_iota(jnp.int32, kr.shape, 1) == ROPE_PAD - 1
    kcat_ref[:, KV_LORA:] = jnp.where(ones_lane, 1.0, kr).astype(BF16)
    qm = z[:, _A_QM:_A_QM + MOBA_WIDTH]
    km = z[:, _A_KM:_A_KM + MOBA_WIDTH]
    vm = z[:, _A_VM:_A_VM + MOBA_WIDTH]
    qm_ref[...] = qm
    km_ref[...] = km
    vm_ref[...] = vm
    lane = lax.broadcasted_iota(jnp.int32, (qm.shape[0], 128), 1)
    lane_lo = lane < MOBA_DH
    zeros = jnp.zeros(lane.shape, F32)
    qp_ref[...] = _pad_heads(qm * MOBA_SCALE, lane_lo, zeros).astype(BF16)
    kp_ref[...] = _pad_heads(km, lane_lo, zeros).astype(BF16)
    vp_ref[...] = _pad_heads(vm, lane_lo, jnp.where(lane == MOBA_PAD - 1, 1.0, 0.0)).astype(BF16)


def _inproj(x2, cs, sn, w, tm):
    n = x2.shape[0]
    tab_blocks = cs.shape[0] // tm
    row = lambda i: (i, 0)
    tab = lambda i: (i % tab_blocks, 0)
    widths = [(MLA_HEADS * QCAT, BF16), (QCAT, BF16), (KV_LORA, F32), (MLA_ROPE, F32),
              (MOBA_WIDTH, F32), (MOBA_WIDTH, F32), (MOBA_WIDTH, F32),
              (MOBA_PADW, BF16), (MOBA_PADW, BF16), (MOBA_PADW, BF16)]
    return pl.pallas_call(
        _inproj_body,
        grid=(n // tm,),
        in_specs=[pl.BlockSpec((tm, D_MODEL), row),
                  pl.BlockSpec((tm, ROPE_PAD), tab), pl.BlockSpec((tm, ROPE_PAD), tab),
                  _full_spec((1, D_MODEL)), _full_spec((D_MODEL, _A_WIDTH)),
                  _full_spec((1, Q_LORA)), _full_spec((Q_LORA, _Q_WIDTH)),
                  _full_spec((1, KV_LORA)), _full_spec((MLA_HEADS // 2, 128, 2 * KV_LORA))],
        out_specs=[pl.BlockSpec((tm, wd), row) for wd, _ in widths],
        out_shape=[jax.ShapeDtypeStruct((n, wd), dt) for wd, dt in widths],
        compiler_params=_cparams("arbitrary"),
        name="inproj",
    )(x2, cs, sn, w["attn_norm"], w["w_a"], w["q_a_norm"], w["w_qb"], w["kv_a_norm"], w["w_uk_pairs"])


def _softmax_step(s, v, m_scr, l_scr, acc_scr):
    m_prev = m_scr[...]
    m_new = jnp.maximum(m_prev, jnp.max(s, axis=-1, keepdims=True))
    alpha = jnp.exp(m_prev - m_new)
    p = jnp.exp(s - m_new)
    l_scr[...] = alpha * l_scr[...] + jnp.sum(p, axis=-1, keepdims=True)
    acc_scr[...] = alpha * acc_scr[...] + jnp.dot(p.astype(BF16), v, preferred_element_type=F32)
    m_scr[...] = m_new


def _mla_prompt_body(qcat_ref, kcat_ref, wuv_ref, out_ref, m_scr, acc_scr, *, tq):
    qi = pl.program_id(1)
    m_scr[...] = jnp.full(m_scr.shape, NEG_INF, F32)
    acc_scr[...] = jnp.zeros(acc_scr.shape, F32)

    def chunk(kc, masked):
        k = kcat_ref[pl.ds(pl.multiple_of(kc * tq, tq), tq), :]
        for h in range(MLA_HEADS):
            s = _dot_nt(qcat_ref[:, h * QCAT:(h + 1) * QCAT], k)
            if masked:
                tok = lax.broadcasted_iota(jnp.int32, s.shape, 0)
                col = lax.broadcasted_iota(jnp.int32, s.shape, 1)
                s = jnp.where(col <= tok, s, NEG_INF)
            m_prev = m_scr[h]
            m_new = jnp.maximum(m_prev, jnp.max(s, axis=-1, keepdims=True))
            p = jnp.exp(s - m_new).astype(BF16)
            acc_scr[h] = jnp.exp(m_prev - m_new) * acc_scr[h] + jnp.dot(p, k, preferred_element_type=F32)
            m_scr[h] = m_new

    def past(kc, carry):
        chunk(kc, False)
        return carry

    lax.fori_loop(0, qi, past, 0)
    chunk(qi, True)
    y = jnp.zeros(out_ref.shape, F32)
    for h in range(MLA_HEADS):
        acc = acc_scr[h]
        o = (acc[:, :KV_LORA] / acc[:, QCAT - 1:QCAT]).astype(BF16)
        y = y + jnp.dot(o, wuv_ref[h], preferred_element_type=F32)
    out_ref[...] = y.astype(out_ref.dtype)


def _mla_prompt(qcat, kcat, wuv_pad, batch, seq, tq):
    nq = seq // tq
    return pl.pallas_call(
        functools.partial(_mla_prompt_body, tq=tq),
        grid=(batch, nq),
        in_specs=[pl.BlockSpec((tq, MLA_HEADS * QCAT), lambda b, i: (b * nq + i, 0)),
                  pl.BlockSpec((seq, QCAT), lambda b, i: (b, 0)),
                  _full_spec((MLA_HEADS, KV_LORA, MLA_HEADS * MLA_V))],
        out_specs=pl.BlockSpec((tq, MLA_HEADS * MLA_V), lambda b, i: (b * nq + i, 0)),
        out_shape=jax.ShapeDtypeStruct((batch * seq, MLA_HEADS * MLA_V), BF16),
        scratch_shapes=[pltpu.VMEM((MLA_HEADS, tq, 1), F32), pltpu.VMEM((MLA_HEADS, tq, QCAT), F32)],
        compiler_params=_cparams("arbitrary", "arbitrary"),
        name="mla_prompt",
    )(qcat, kcat, wuv_pad)


def _alibi_slope(h):
    return 2.0 ** (-8.0 * (h + 1) / MOBA_HEADS)


def _block_choice(scores, n_valid):
    lane = lax.broadcasted_iota(jnp.int32, scores.shape, 1)
    n = lane & 7
    s = jnp.where(n < n_valid, scores, NEG_INF)
    beaten = jnp.zeros(scores.shape, F32)
    for d in range(1, 8):
        other = jnp.where(n >= d, pltpu.roll(s, d, axis=1), pltpu.roll(s, 128 - 8 + d, axis=1))
        wins = (other > s) | ((other == s) & (n >= d))
        beaten = beaten + wins.astype(F32)
    return jnp.where((beaten < MOBA_TOPK) & (n < n_valid), 1.0, 0.0)


def _moba_prompt_body(qf_ref, kf_ref, qp_ref, kp_ref, vp_ref, out_ref, kmean_scr, sel_scr, m_scr, acc_scr):
    qi = pl.program_id(1)
    j = pl.program_id(2)
    nblk = kmean_scr.shape[0]
    blk = MOBA_BLOCK

    @pl.when((qi == 0) & (j == 0))
    def _():
        kmean_scr[...] = jnp.zeros(kmean_scr.shape, F32)

    @pl.when(j == 0)
    def _():
        m_scr[...] = jnp.full(m_scr.shape, NEG_INF, F32)
        acc_scr[...] = jnp.zeros(acc_scr.shape, F32)
        km = kmean_scr[...]
        tiled = jnp.concatenate([km] * MOBA_HEADS + [jnp.zeros((128 - MOBA_HEADS * nblk, MOBA_WIDTH), F32)], axis=0)
        row_h = lax.broadcasted_iota(jnp.int32, tiled.shape, 0) // nblk
        lane_h = lax.broadcasted_iota(jnp.int32, tiled.shape, 1) // MOBA_DH
        kmbd = jnp.where(row_h == lane_h, tiled, 0.0)
        scores = lax.dot_general(qf_ref[...], kmbd, (((1,), (1,)), ((), ())),
                                 preferred_element_type=F32, precision=lax.Precision.HIGHEST)
        sel_scr[...] = _block_choice(scores, qi)

    @pl.when(j <= qi)
    def _():
        diag = (j == qi).astype(F32)
        r = lax.broadcasted_iota(jnp.int32, (blk, blk), 0)
        c = lax.broadcasted_iota(jnp.int32, (blk, blk), 1)
        dist = ((qi - j) * blk + (r - c)).astype(F32)
        causal = jnp.where(r >= c, diag, 0.0)
        er = lax.broadcasted_iota(jnp.int32, (128, MOBA_PADW), 0)
        ec = lax.broadcasted_iota(jnp.int32, (128, MOBA_PADW), 1)
        spread = jnp.where(((er >> 3) == (ec >> 7)) & ((er & 7) == j), 1.0 - diag, 0.0).astype(BF16)
        chosen = jnp.dot(sel_scr[...].astype(BF16), spread, preferred_element_type=F32)
        for h in range(MOBA_HEADS):
            sl = slice(h * MOBA_PAD, (h + 1) * MOBA_PAD)
            s = _dot_nt(qp_ref[:, sl], kp_ref[:, sl])
            s = s - _alibi_slope(h) * dist
            allowed = causal + jnp.concatenate([chosen[:, sl]] * (blk // MOBA_PAD), axis=1)
            s = jnp.where(allowed > 0.5, s, NEG_INF)
            m_prev = m_scr[h]
            m_new = jnp.maximum(m_prev, jnp.max(s, axis=-1, keepdims=True))
            p = jnp.exp(s - m_new).astype(BF16)
            acc_scr[h] = jnp.exp(m_prev - m_new) * acc_scr[h] + jnp.dot(p, vp_ref[:, sl], preferred_element_type=F32)
            m_scr[h] = m_new

    @pl.when(j == qi)
    def _():
        kmean_scr[pl.ds(qi, 1), :] = jnp.mean(kf_ref[...], axis=0, keepdims=True)
        head_lane = lax.broadcasted_iota(jnp.int32, (blk, MOBA_PAD), 1) < MOBA_DH
        for h in range(MOBA_HEADS):
            acc = acc_scr[h]
            o = jnp.where(head_lane, acc / acc[:, MOBA_PAD - 1:MOBA_PAD], 0.0)
            out_ref[:, h * MOBA_PAD:(h + 1) * MOBA_PAD] = o.astype(out_ref.dtype)


def _moba_prompt(qm, km, qp, kp, vp, batch, seq):
    nb = seq // MOBA_BLOCK
    assert nb <= 8 and seq % MOBA_BLOCK == 0
    qmap = lambda b, i, j: (b * nb + i, 0)
    kmap = lambda b, i, j: (b * nb + jnp.minimum(i, j), 0)
    blk = MOBA_BLOCK
    return pl.pallas_call(
        _moba_prompt_body,
        grid=(batch, nb, nb),
        in_specs=[pl.BlockSpec((blk, MOBA_WIDTH), qmap), pl.BlockSpec((blk, MOBA_WIDTH), qmap),
                  pl.BlockSpec((blk, MOBA_PADW), qmap),
                  pl.BlockSpec((blk, MOBA_PADW), kmap), pl.BlockSpec((blk, MOBA_PADW), kmap)],
        out_specs=pl.BlockSpec((blk, MOBA_PADW), qmap),
        out_shape=jax.ShapeDtypeStruct((batch * seq, MOBA_PADW), BF16),
        scratch_shapes=[pltpu.VMEM((8, MOBA_WIDTH), F32), pltpu.VMEM((blk, 128), F32),
                        pltpu.VMEM((MOBA_HEADS, blk, 1), F32), pltpu.VMEM((MOBA_HEADS, blk, MOBA_PAD), F32)],
        compiler_params=_cparams("arbitrary", "arbitrary", "arbitrary"),
        name="moba_prompt",
    )(qm, km, qp, kp, vp)


def _sigmoid(g):
    return 1.0 / (1.0 + jnp.exp(-g))


def _merge_body(x_ref, ya_ref, ob_ref, an_ref, wg_ref, wua_ref, wub_ref, wo_ref, h_ref):
    x = x_ref[...]
    xn = _rms(x, an_ref[...]).astype(BF16)
    g = jnp.dot(xn, wg_ref[...], preferred_element_type=F32)
    y_a = jnp.dot(ya_ref[...], wua_ref[...], preferred_element_type=F32)
    y_b = jnp.dot(ob_ref[...], wub_ref[...], preferred_element_type=F32)
    mix = _sigmoid(g[:, :D_MODEL]) * y_a + _sigmoid(g[:, D_MODEL:]) * y_b
    h_ref[...] = x + jnp.dot(mix.astype(BF16), wo_ref[...], preferred_element_type=F32)


def _merge(x2, ya, ob, w_up_b, w, tm):
    n = x2.shape[0]
    wb = ob.shape[1]
    row = lambda i: (i, 0)
    return pl.pallas_call(
        _merge_body,
        grid=(n // tm,),
        in_specs=[pl.BlockSpec((tm, D_MODEL), row), pl.BlockSpec((tm, MLA_HEADS * MLA_V), row),
                  pl.BlockSpec((tm, wb), row),
                  _full_spec((1, D_MODEL)), _full_spec((D_MODEL, 2 * D_MODEL)),
                  _full_spec((MLA_HEADS * MLA_V, D_MODEL)), _full_spec((wb, D_MODEL)),
                  _full_spec((D_MODEL, D_MODEL))],
        out_specs=pl.BlockSpec((tm, D_MODEL), row),
        out_shape=jax.ShapeDtypeStruct((n, D_MODEL), F32),
        compiler_params=_cparams("arbitrary"),
        name="merge",
    )(x2, ya, ob, w["attn_norm"], w["w_g"], w["w_up_a"], w_up_b, w["w_o"])


def _topk_extract(s, key):
    rank = jnp.full(s.shape, float(PEER_TOPK), F32)
    work = s
    vals = []
    for r in range(PEER_TOPK):
        m = jnp.max(work, axis=0, keepdims=True)
        if key is None:
            hit = work == m
        else:
            hit = key == jnp.min(jnp.where(work == m, key, 1e9), axis=0, keepdims=True)
        rank = jnp.where(hit, float(r), rank)
        work = jnp.where(hit, -jnp.inf, work)
        vals.append(m)
    return rank, vals


def _topk_ranks(s, key, rank_scr, vals_scr):
    rank, vals = _topk_extract(s, None)
    rank_scr[...] = rank
    vals_scr[...] = jnp.concatenate(vals, axis=0)
    marked = jnp.sum(jnp.where(rank < float(PEER_TOPK), 1.0, 0.0))

    @pl.when(marked != float(PEER_TOPK * s.shape[1]))
    def _():
        rank_k, vals_k = _topk_extract(s, key)
        rank_scr[...] = rank_k
        vals_scr[...] = jnp.concatenate(vals_k, axis=0)

    return rank_scr[...], [vals_scr[r:r + 1, :] for r in range(PEER_TOPK)]


def _gelu(x):
    return 0.5 * x * (1.0 + lax.erf(x * np.float32(np.sqrt(0.5))))


def _peer_body(h_ref, fn_ref, wq_ref, k1_ref, k2_ref, u_ref, vt_ref, on_ref, y_ref,
               xn_scr, qt_scr, r2_scr, e2_scr, nb_scr, p1_scr, acc_scr,
               rk1_scr, rk2_scr, rkc_scr, v1_scr, v2_scr, vc_scr):
    c = pl.program_id(1)
    t = h_ref.shape[0]
    kk = PEER_TOPK

    @pl.when(c == 0)
    def _():
        xn = _rms(h_ref[...], fn_ref[...]).astype(BF16)
        xn_scr[...] = xn
        qt_scr[...] = _dot_nt(wq_ref[...], xn)
        acc_scr[...] = jnp.zeros(acc_scr.shape, F32)
        key128 = lax.broadcasted_iota(jnp.int32, (PEER_NKEYS, t), 0).astype(F32)
        r16 = lax.broadcasted_iota(jnp.int32, (kk, t), 0).astype(F32)
        r8 = lax.broadcasted_iota(jnp.int32, (8, t), 0).astype(F32)

        def per_head(hd, carry):
            base = pl.multiple_of(hd * PEER_DKEY, PEER_DKEY)
            q1 = qt_scr[pl.ds(base, PEER_DKEY // 2), :].astype(BF16)
            q2 = qt_scr[pl.ds(base + PEER_DKEY // 2, PEER_DKEY // 2), :].astype(BF16)
            s1 = jnp.dot(k1_ref[hd], q1, preferred_element_type=F32)
            s2 = jnp.dot(k2_ref[hd], q2, preferred_element_type=F32)
            rank1, v1r = _topk_ranks(s1, key128, rk1_scr, v1_scr)
            rank2, v2r = _topk_ranks(s2, key128, rk2_scr, v2_scr)
            v1, v1h = jnp.concatenate(v1r, axis=0), jnp.concatenate(v1r[:8], axis=0)
            v2, v2h = jnp.concatenate(v2r, axis=0), jnp.concatenate(v2r[:8], axis=0)
            slabs = [(v1 + v2r[0], r16 * kk, None),
                     (v1h + v2r[1], r8 * kk + 1.0, None),
                     (v2 + v1r[0], r16, r16 >= 2.0),
                     (v2h + v1r[1], r8 + float(kk), r8 >= 2.0)]
            for b in (2, 3, 4):
                slabs.append((v1h + v2r[b], r8 * kk + float(b), (r8 >= 2.0) & ((r8 + 1.0) * (b + 1) <= kk)))
            cand = jnp.concatenate([vals if ok is None else jnp.where(ok, vals, -jnp.inf) for vals, _, ok in slabs],
                                   axis=0)
            ckey = jnp.concatenate([key if ok is None else jnp.where(ok, key, 1e6) for _, key, ok in slabs], axis=0)
            rankc, _ = _topk_ranks(cand, ckey, rkc_scr, vc_scr)
            selc = jnp.where(rankc < float(kk), 1.0, 0.0)
            sel_a, sel_b, sel_c, sel_d = selc[0:16], selc[16:24], selc[24:40], selc[40:48]
            sel_e = [selc[48:56], selc[56:64], selc[64:72]]
            e1, e1h = jnp.exp(v1 - v1r[0]), jnp.exp(v1h - v1r[0])
            e2, e2h = jnp.exp(v2 - v2r[0]), jnp.exp(v2h - v2r[0])
            e1r = [jnp.exp(v1r[a] - v1r[0]) for a in range(kk)]
            e2r = [jnp.exp(v2r[b] - v2r[0]) for b in range(kk)]
            colsum = lambda x: jnp.sum(x, axis=0, keepdims=True)
            zero8 = jnp.zeros((8, t), F32)
            nb = (sel_a + jnp.concatenate([sel_b + sel_e[0] + sel_e[1] + sel_e[2], zero8], axis=0)
                  + jnp.where(r16 == 0.0, colsum(sel_c), 0.0) + jnp.where(r16 == 1.0, colsum(sel_d), 0.0))
            zsum = (colsum(sel_a * e1) * e2r[0] + colsum(sel_b * e1h) * e2r[1]
                    + colsum(sel_c * e2) * e1r[0] + colsum(sel_d * e2h) * e1r[1])
            for n, b in enumerate((2, 3, 4)):
                zsum = zsum + colsum(sel_e[n] * e1h) * e2r[b]
            e2r = [e / zsum for e in e2r]
            nbx = jnp.zeros((PEER_NKEYS, t), F32)
            p1x = jnp.zeros((PEER_NKEYS, t), F32)
            e2x = jnp.zeros((PEER_NKEYS, t), F32)
            for a in range(kk):
                hit1 = rank1 == float(a)
                nbx = jnp.where(hit1, nb[a:a + 1], nbx)
                p1x = jnp.where(hit1, e1r[a], p1x)
                e2x = jnp.where(rank2 == float(a), e2r[a], e2x)
            r2_scr[hd] = rank2.astype(BF16)
            e2_scr[hd] = e2x.astype(BF16)
            nb_scr[hd] = nbx
            p1_scr[hd] = p1x
            return carry

        lax.fori_loop(0, PEER_HEADS, per_head, 0)

    xn = xn_scr[...]
    acc = None
    for kt in range(PEER_ECHUNK // PEER_KTILE):
        rows = slice(kt * PEER_KTILE, (kt + 1) * PEER_KTILE)
        act = _dot_nt(u_ref[rows, :], xn)
        parts = []
        for ii in range(PEER_KTILE // PEER_NKEYS):
            i = c * PEER_ICHUNK + kt * (PEER_KTILE // PEER_NKEYS) + ii
            wgt = jnp.zeros((PEER_NKEYS, t), BF16)
            for hd in range(PEER_HEADS):
                nbrow = nb_scr[hd, pl.ds(i, 1), :].astype(BF16)
                p1row = p1_scr[hd, pl.ds(i, 1), :].astype(BF16)
                wgt = wgt + jnp.where(r2_scr[hd] < nbrow, e2_scr[hd] * p1row, jnp.zeros((), BF16))
            parts.append(wgt * _gelu(act[ii * PEER_NKEYS:(ii + 1) * PEER_NKEYS]).astype(BF16))
        piece = jnp.dot(vt_ref[:, rows], jnp.concatenate(parts, axis=0), preferred_element_type=F32)
        acc = piece if acc is None else acc + piece
    acc_scr[...] += acc

    @pl.when(c == pl.num_programs(1) - 1)
    def _():
        h2 = h_ref[...] + acc_scr[...].T
        y_ref[...] = _rms(h2, on_ref[...])


def _peer(h2d, w, t):
    n = h2d.shape[0]
    n_exp = w["peer_u"].shape[0]
    nchunk = n_exp // PEER_ECHUNK
    slab = (PEER_HEADS, PEER_NKEYS, t)
    return pl.pallas_call(
        _peer_body,
        grid=(n // t, nchunk),
        in_specs=[pl.BlockSpec((t, D_MODEL), lambda i, c: (i, 0)),
                  _full_spec((1, D_MODEL)), _full_spec((PEER_HEADS * PEER_DKEY, D_MODEL)),
                  _full_spec((PEER_HEADS, PEER_NKEYS, PEER_DKEY // 2)),
                  _full_spec((PEER_HEADS, PEER_NKEYS, PEER_DKEY // 2)),
                  pl.BlockSpec((PEER_ECHUNK, D_MODEL), lambda i, c: (c, 0)),
                  pl.BlockSpec((D_MODEL, PEER_ECHUNK), lambda i, c: (0, c)),
                  _full_spec((1, D_MODEL))],
        out_specs=pl.BlockSpec((t, D_MODEL), lambda i, c: (i, 0)),
        out_shape=jax.ShapeDtypeStruct((n, D_MODEL), F32),
        scratch_shapes=[pltpu.VMEM((t, D_MODEL), BF16), pltpu.VMEM((PEER_HEADS * PEER_DKEY, t), F32),
                        pltpu.VMEM(slab, BF16), pltpu.VMEM(slab, BF16), pltpu.VMEM(slab, F32), pltpu.VMEM(slab, F32),
                        pltpu.VMEM((D_MODEL, t), F32),
                        pltpu.VMEM((PEER_NKEYS, t), F32), pltpu.VMEM((PEER_NKEYS, t), F32),
                        pltpu.VMEM((PEER_NCAND, t), F32)] + [pltpu.VMEM((PEER_TOPK, t), F32)] * 3,
        compiler_params=_cparams("arbitrary", "arbitrary"),
        name="peer",
    )(h2d, w["ffn_norm"], w["peer_wq_t"], w["peer_k1"], w["peer_k2"], w["peer_u"], w["peer_vt"], w["final_norm"])


SAMPLE_PAGES_PER_STEP = 16
MLA_PAGES_PER_STEP = 32


def _mla_sample_body(pt_ref, q_ref, knew_ref, *rest, pages, dec):
    c_refs = rest[:pages]
    r_refs = rest[pages:2 * pages]
    wuv_ref, out_ref, q_scr, knew_scr, m_scr, l_scr, acc_scr = rest[2 * pages:]
    s_id = pl.program_id(1)
    rows = MLA_HEADS * dec

    @pl.when(s_id == 0)
    def _():
        for h in range(MLA_HEADS):
            q_scr[h * dec:(h + 1) * dec, :] = q_ref[0, :, h * QCAT:(h + 1) * QCAT].astype(F32)
        m_scr[...] = jnp.full(m_scr.shape, NEG_INF, F32)
        l_scr[...] = jnp.zeros(l_scr.shape, F32)
        acc_scr[...] = jnp.zeros(acc_scr.shape, F32)

    q = q_scr[...].astype(BF16)
    q_lat = q[:, :KV_LORA]
    q_rope = q[:, KV_LORA:KV_LORA + MLA_ROPE]
    cs = [c_refs[i][0].astype(BF16) for i in range(pages)]
    s = jnp.concatenate([_dot_nt(q_lat, cs[i])
                         + jnp.dot(q_rope, r_refs[i][0].astype(BF16), preferred_element_type=F32)
                         for i in range(pages)], axis=1)
    m_prev = m_scr[...]
    m_new = jnp.maximum(m_prev, jnp.max(s, axis=-1, keepdims=True))
    alpha = jnp.exp(m_prev - m_new)
    p = jnp.exp(s - m_new)
    l_scr[...] = alpha * l_scr[...] + jnp.sum(p, axis=-1, keepdims=True)
    p = p.astype(BF16)
    pv = jnp.zeros(acc_scr.shape, F32)
    for i in range(pages):
        pv = pv + jnp.dot(p[:, i * PAGE_SIZE:(i + 1) * PAGE_SIZE], cs[i], preferred_element_type=F32)
    acc_scr[...] = alpha * acc_scr[...] + pv
    m_scr[...] = m_new

    @pl.when(s_id == pl.num_programs(1) - 1)
    def _():
        knew_scr[...] = jnp.zeros(knew_scr.shape, F32)
        knew_scr[0:dec, :] = knew_ref[0].astype(F32)
        k = knew_scr[...].astype(BF16)
        s_new = _dot_nt(q, k)
        tok = lax.broadcasted_iota(jnp.int32, s_new.shape, 0) & (dec - 1)
        col = lax.broadcasted_iota(jnp.int32, s_new.shape, 1)
        s_new = jnp.where(col <= tok, s_new, NEG_INF)
        _softmax_step(s_new, k[:, :KV_LORA], m_scr, l_scr, acc_scr)
        o = (acc_scr[...] / l_scr[...]).astype(BF16)
        y = jnp.zeros((dec, MLA_HEADS * MLA_V), F32)
        for h in range(MLA_HEADS):
            y = y + jnp.dot(o[h * dec:(h + 1) * dec], wuv_ref[h], preferred_element_type=F32)
        out_ref[0] = y.astype(out_ref.dtype)


def _mla_sample(page_flat, qcat3, kcat3, cache_c, cache_r, wuv_pad, n_pages):
    bd, dec, _ = qcat3.shape
    assert dec & (dec - 1) == 0 and dec <= PAGE_SIZE
    pages = min(MLA_PAGES_PER_STEP, n_pages)
    steps = n_pages // pages
    rows = MLA_HEADS * dec

    def page_map(i):
        return lambda b, s, pt: (pt[b * n_pages + s * pages + i], 0, 0)

    per_b = lambda b, s, pt: (b, 0, 0)
    grid_spec = pltpu.PrefetchScalarGridSpec(
        num_scalar_prefetch=1,
        grid=(bd, steps),
        in_specs=([pl.BlockSpec((1, dec, MLA_HEADS * QCAT), per_b), pl.BlockSpec((1, dec, QCAT), per_b)]
                  + [pl.BlockSpec((1, PAGE_SIZE, KV_LORA), page_map(i)) for i in range(pages)]
                  + [pl.BlockSpec((1, MLA_ROPE, PAGE_SIZE), page_map(i)) for i in range(pages)]
                  + [pl.BlockSpec((MLA_HEADS, KV_LORA, MLA_HEADS * MLA_V), lambda b, s, pt: (0, 0, 0))]),
        out_specs=pl.BlockSpec((1, dec, MLA_HEADS * MLA_V), per_b),
        scratch_shapes=[pltpu.VMEM((rows, QCAT), F32), pltpu.VMEM((PAGE_SIZE, QCAT), F32),
                        pltpu.VMEM((rows, 1), F32), pltpu.VMEM((rows, 1), F32), pltpu.VMEM((rows, KV_LORA), F32)],
    )
    return pl.pallas_call(
        functools.partial(_mla_sample_body, pages=pages, dec=dec),
        grid_spec=grid_spec,
        out_shape=jax.ShapeDtypeStruct((bd, dec, MLA_HEADS * MLA_V), BF16),
        compiler_params=_cparams("arbitrary", "arbitrary"),
        name="mla_sample",
    )(page_flat, qcat3, kcat3, *([cache_c] * pages), *([cache_r] * pages), wuv_pad)


def _group_choice(scores, group, n_valid, topk):
    lane = lax.broadcasted_iota(jnp.int32, scores.shape, 1)
    s = jnp.where(lane < n_valid, scores, NEG_INF)
    beaten = jnp.zeros(scores.shape, F32)
    for d in range(1, group):
        other = jnp.where(lane >= d, pltpu.roll(s, d, axis=1), pltpu.roll(s, 128 - group + d, axis=1))
        wins = (other > s) | ((other == s) & (lane >= d))
        beaten = beaten + wins.astype(F32)
    return jnp.where((beaten < topk) & (lane < n_valid), 1.0, 0.0)


def _moba_sample_body(pt_ref, qm_ref, kn_ref, vn_ref, e_ref, *rest, pages, dec, n_pages):
    k_refs = rest[:pages]
    v_refs = rest[pages:2 * pages]
    out_ref, qbd_scr, kmt_scr, s_scr, snew_scr, new_scr, l_scr, acc_scr = rest[2 * pages:]
    s_id = pl.program_id(1)
    n_k = n_pages // pages
    rows = MOBA_HEADS * dec
    n_blocks = n_pages // 2
    past = n_pages * PAGE_SIZE
    row = lax.broadcasted_iota(jnp.int32, (rows, PAGE_SIZE), 0)
    col = lax.broadcasted_iota(jnp.int32, (rows, PAGE_SIZE), 1)
    tok = row & (dec - 1)
    slope = jnp.zeros((rows, PAGE_SIZE), F32)
    for h in range(MOBA_HEADS):
        slope = jnp.where(row // dec == h, _alibi_slope(h), slope)

    @pl.when(s_id == 0)
    def _():
        tiled = jnp.concatenate([qm_ref[0]] * MOBA_HEADS, axis=0)
        row_h = lax.broadcasted_iota(jnp.int32, tiled.shape, 0) // dec
        lane_h = lax.broadcasted_iota(jnp.int32, tiled.shape, 1) // MOBA_DH
        qbd_scr[...] = jnp.where(row_h == lane_h, tiled * MOBA_SCALE, 0.0)
        kmt_scr[...] = jnp.zeros(kmt_scr.shape, F32)

    @pl.when(s_id < n_k)
    def _():
        qbd = qbd_scr[...].astype(BF16)
        lane = lax.broadcasted_iota(jnp.int32, kmt_scr.shape, 1)
        for i in range(0, pages, 2):
            kt0 = k_refs[i][0]
            kt1 = k_refs[i + 1][0]
            page = s_id * pages + i
            s_scr[page] = jnp.dot(qbd, kt0.astype(BF16), preferred_element_type=F32)
            s_scr[page + 1] = jnp.dot(qbd, kt1.astype(BF16), preferred_element_type=F32)
            mean = 0.5 * jnp.mean(kt0 + kt1, axis=1, keepdims=True)
            kmt_scr[...] = jnp.where(lane == page // 2, mean, kmt_scr[...])

    @pl.when(s_id == n_k)
    def _():
        scores = jnp.dot(qbd_scr[...], kmt_scr[...], preferred_element_type=F32, precision=lax.Precision.HIGHEST)
        sel = _group_choice(scores, max(n_blocks, 1), n_blocks, MOBA_TOPK)
        chosen = jnp.dot(sel.astype(BF16), e_ref[...], preferred_element_type=F32)
        new_scr[...] = jnp.zeros(new_scr.shape, F32)
        new_scr[0:dec, :] = kn_ref[0]
        s_new = _dot_nt(qbd_scr[...].astype(BF16), new_scr[...].astype(BF16))
        s_new = s_new - slope * (tok - col).astype(F32)
        s_new = jnp.where(col <= tok, s_new, NEG_INF)
        bias = slope * (past + tok - col).astype(F32)
        mx = s_new
        for page in range(n_pages):
            cols = slice(page * PAGE_SIZE, (page + 1) * PAGE_SIZE)
            sm = jnp.where(chosen[:, cols] > 0.5, s_scr[page] - (bias - slope * float(page * PAGE_SIZE)), NEG_INF)
            s_scr[page] = sm
            mx = jnp.maximum(mx, sm)
        m = jnp.max(mx, axis=-1, keepdims=True)
        p_new = jnp.exp(s_new - m)
        snew_scr[...] = p_new
        lsum = p_new
        for page in range(n_pages):
            p = jnp.exp(s_scr[page] - m)
            s_scr[page] = p
            lsum = lsum + p
        l_scr[...] = jnp.sum(lsum, axis=-1, keepdims=True)
        acc_scr[...] = jnp.zeros(acc_scr.shape, F32)

    @pl.when(s_id >= n_k)
    def _():
        pv = jnp.zeros(acc_scr.shape, F32)
        for i in range(pages):
            page = (s_id - n_k) * pages + i
            pv = pv + _dot_nt(s_scr[page].astype(BF16), v_refs[i][0].astype(BF16))
        acc_scr[...] += pv

    @pl.when(s_id == 2 * n_k - 1)
    def _():
        new_scr[...] = jnp.zeros(new_scr.shape, F32)
        new_scr[0:dec, :] = vn_ref[0]
        acc = acc_scr[...] + jnp.dot(snew_scr[...].astype(BF16), new_scr[...].astype(BF16),
                                     preferred_element_type=F32)
        o = acc / l_scr[...]
        row_h = lax.broadcasted_iota(jnp.int32, o.shape, 0) // dec
        lane_h = lax.broadcasted_iota(jnp.int32, o.shape, 1) // MOBA_DH
        o = jnp.where(row_h == lane_h, o, 0.0)
        y = o[0:dec]
        for h in range(1, MOBA_HEADS):
            y = y + o[h * dec:(h + 1) * dec]
        out_ref[0] = y.astype(out_ref.dtype)


def _moba_sample(page_flat, qm3, km3, vm3, cache_kt, cache_vt, n_pages):
    bd, dec, _ = qm3.shape
    assert dec == 8, "row groups of the stacked heads must be one sublane tile"
    assert n_pages % 2 == 0, "the past must be a whole number of MoBA blocks"
    n_blocks = n_pages // 2
    assert n_blocks & (n_blocks - 1) == 0 and n_blocks <= 128
    pages = min(SAMPLE_PAGES_PER_STEP, n_pages)
    n_k = n_pages // pages
    rows = MOBA_HEADS * dec

    def k_map(i):
        return lambda b, s, pt: (pt[b * n_pages + jnp.minimum(s, n_k - 1) * pages + i], 0, 0)

    def v_map(i):
        return lambda b, s, pt: (pt[b * n_pages + jnp.maximum(s - n_k, 0) * pages + i], 0, 0)

    per_b = lambda b, s, pt: (b, 0, 0)
    n_keys = n_pages * PAGE_SIZE
    member = (jnp.arange(n_keys)[None, :] // MOBA_BLOCK == jnp.arange(128)[:, None]).astype(BF16)
    grid_spec = pltpu.PrefetchScalarGridSpec(
        num_scalar_prefetch=1,
        grid=(bd, 2 * n_k),
        in_specs=([pl.BlockSpec((1, dec, MOBA_WIDTH), per_b)] * 3
                  + [pl.BlockSpec((128, n_keys), lambda b, s, pt: (0, 0))]
                  + [pl.BlockSpec((1, MOBA_WIDTH, PAGE_SIZE), k_map(i)) for i in range(pages)]
                  + [pl.BlockSpec((1, MOBA_WIDTH, PAGE_SIZE), v_map(i)) for i in range(pages)]),
        out_specs=pl.BlockSpec((1, dec, MOBA_WIDTH), per_b),
        scratch_shapes=[pltpu.VMEM((rows, MOBA_WIDTH), F32), pltpu.VMEM((MOBA_WIDTH, 128), F32),
                        pltpu.VMEM((n_pages, rows, PAGE_SIZE), F32), pltpu.VMEM((rows, PAGE_SIZE), F32),
                        pltpu.VMEM((PAGE_SIZE, MOBA_WIDTH), F32),
                        pltpu.VMEM((rows, 1), F32), pltpu.VMEM((rows, MOBA_WIDTH), F32)],
    )
    return pl.pallas_call(
        functools.partial(_moba_sample_body, pages=pages, dec=dec, n_pages=n_pages),
        grid_spec=grid_spec,
        out_shape=jax.ShapeDtypeStruct((bd, dec, MOBA_WIDTH), BF16),
        compiler_params=_cparams("arbitrary", "arbitrary"),
        name="moba_sample",
    )(page_flat, qm3, km3, vm3, member, *([cache_kt] * pages), *([cache_vt] * pages))


def _rope_tables(pos):
    half = MLA_ROPE // 2
    inv = ROPE_THETA ** (-jnp.arange(half, dtype=F32) / half)
    ang = pos.astype(F32)[:, None] * inv
    cos, sin = jnp.cos(ang), jnp.sin(ang)
    pad = jnp.zeros((pos.shape[0], ROPE_PAD - MLA_ROPE), F32)
    return jnp.concatenate([cos, cos, pad], axis=1), jnp.concatenate([-sin, sin, pad], axis=1)


def _swap_halves(w):
    half = MLA_ROPE // 2
    return jnp.concatenate([w[..., half:], w[..., :half]], axis=-1)


def _layer_weights(l, attn_norm, w_in, q_a_norm, w_q_b, kv_a_norm, w_uk, w_uv, w_up_a, w_up_b, w_o,
                   ffn_norm, peer_w_query, peer_keys_1, peer_keys_2, peer_u, peer_v, final_norm):
    sizes = (Q_LORA, KV_LORA, MLA_ROPE, MOBA_WIDTH, MOBA_WIDTH, MOBA_WIDTH, D_MODEL, D_MODEL)
    qa, ckv, kr, qm, km, vm, ga, gb = jnp.split(w_in[l], [int(v) for v in np.cumsum(sizes)[:-1]], axis=1)
    pad = jnp.zeros((D_MODEL, ROPE_PAD - MLA_ROPE), F32)
    w_a = jnp.concatenate([qa, ckv, qm, km, vm, kr, pad, _swap_halves(kr), pad], axis=1)
    wq = w_q_b[l].reshape(Q_LORA, MLA_HEADS, MLA_NOPE + MLA_ROPE)
    nope = wq[:, :, :MLA_NOPE].reshape(Q_LORA, MLA_HEADS * MLA_NOPE)
    rope = wq[:, :, MLA_NOPE:]
    padq = ((0, 0), (0, 0), (0, ROPE_PAD - MLA_ROPE))
    rope_p = jnp.pad(rope, padq).reshape(Q_LORA, MLA_HEADS * ROPE_PAD)
    ropes_p = jnp.pad(_swap_halves(rope), padq).reshape(Q_LORA, MLA_HEADS * ROPE_PAD)
    w_qb = jnp.concatenate([nope, rope_p, ropes_p], axis=1)
    wuk_t = jnp.transpose(w_uk[l], (1, 2, 0))
    z = jnp.zeros((MLA_NOPE, KV_LORA), F32)
    pairs = [jnp.concatenate([jnp.concatenate([wuk_t[2 * p], z], axis=1),
                              jnp.concatenate([z, wuk_t[2 * p + 1]], axis=1)], axis=0)
             for p in range(MLA_HEADS // 2)]
    wuv_t = jnp.transpose(w_uv[l], (1, 0, 2))
    wuv_pad = jnp.stack([jnp.pad(wuv_t[h], ((0, 0), (h * MLA_V, (MLA_HEADS - 1 - h) * MLA_V)))
                         for h in range(MLA_HEADS)])
    wub = w_up_b[l].reshape(MOBA_HEADS, MOBA_DH, D_MODEL)
    wub_pad = jnp.pad(wub, ((0, 0), (0, MOBA_PAD - MOBA_DH), (0, 0))).reshape(MOBA_PADW, D_MODEL)
    return {
        "attn_norm": attn_norm[l][None, :], "w_a": w_a.astype(BF16),
        "q_a_norm": q_a_norm[l][None, :], "w_qb": w_qb.astype(BF16),
        "kv_a_norm": kv_a_norm[l][None, :], "w_uk_pairs": jnp.stack(pairs).astype(BF16),
        "wuv_pad": wuv_pad.astype(BF16),
        "w_g": jnp.concatenate([ga, gb], axis=1).astype(BF16),
        "w_up_a": w_up_a[l].astype(BF16), "w_up_b": w_up_b[l].astype(BF16), "w_up_b_pad": wub_pad.astype(BF16),
        "w_o": w_o[l].astype(BF16),
        "ffn_norm": ffn_norm[l][None, :], "peer_wq_t": peer_w_query[l].T.astype(BF16),
        "peer_k1": peer_keys_1[l].astype(BF16), "peer_k2": peer_keys_2[l].astype(BF16),
        "peer_u": peer_u[l].astype(BF16), "peer_vt": peer_v[l].T.astype(BF16),
        "final_norm": final_norm[None, :],
    }


def _token_tile(n, want):
    t = min(want, n)
    assert n % t == 0
    return t


def kernel(x_prompt, x_sample, cache_mla_latent, cache_mla_rope, cache_moba_k, cache_moba_v, page_table,
           attn_norm, w_in, q_a_norm, w_q_b, kv_a_norm, w_uk, w_uv, w_up_a, w_up_b, w_o,
           ffn_norm, peer_w_query, peer_keys_1, peer_keys_2, peer_u, peer_v, final_norm):
    batch, seq, _ = x_prompt.shape
    bd, dec, _ = x_sample.shape
    depth = w_in.shape[0]
    n_pages = page_table.shape[1]
    past = n_pages * PAGE_SIZE
    n_p, n_s = batch * seq, bd * dec
    tm_p = _token_tile(seq, 256)
    tm_s = _token_tile(n_s, 256)
    assert tm_s % dec == 0
    cs_p, sn_p = _rope_tables(jnp.arange(seq))
    cs_s, sn_s = _rope_tables(past + jnp.arange(dec))
    cs_s, sn_s = jnp.tile(cs_s, (tm_s // dec, 1)), jnp.tile(sn_s, (tm_s // dec, 1))
    page_flat = page_table.reshape(-1).astype(jnp.int32)

    assert depth == 1, "the PEER kernel applies the final norm, so it closes the only layer"
    l = 0
    w = _layer_weights(l, attn_norm, w_in, q_a_norm, w_q_b, kv_a_norm, w_uk, w_uv, w_up_a, w_up_b, w_o,
                       ffn_norm, peer_w_query, peer_keys_1, peer_keys_2, peer_u, peer_v, final_norm)
    hp = x_prompt.reshape(n_p, D_MODEL)
    qcat, kcat, c, kr, qm, km, vm, qp, kp, vp = _inproj(hp, cs_p, sn_p, w, tm_p)
    ya = _mla_prompt(qcat, kcat, w["wuv_pad"], batch, seq, _token_tile(seq, 256))
    ob = _moba_prompt(qm, km, qp, kp, vp, batch, seq)
    y_p = _peer(_merge(hp, ya, ob, w["w_up_b_pad"], w, tm_p), w, _token_tile(n_p, 256))
    outs_p = (c.reshape(1, batch, seq, KV_LORA), kr.reshape(1, batch, seq, MLA_ROPE),
              km.reshape(1, batch, seq, MOBA_HEADS, MOBA_DH), vm.reshape(1, batch, seq, MOBA_HEADS, MOBA_DH))
    hs = x_sample.reshape(n_s, D_MODEL)
    qcat, kcat, c, kr, qm, km, vm, _, _, _ = _inproj(hs, cs_s, sn_s, w, tm_s)
    rope_t = jnp.swapaxes(cache_mla_rope[l], 1, 2)
    moba_kt = jnp.transpose(cache_moba_k[l], (0, 2, 3, 1)).reshape(-1, MOBA_WIDTH, PAGE_SIZE)
    moba_vt = jnp.transpose(cache_moba_v[l], (0, 2, 3, 1)).reshape(-1, MOBA_WIDTH, PAGE_SIZE)
    ya = _mla_sample(page_flat, qcat.reshape(bd, dec, -1), kcat.reshape(bd, dec, -1),
                     cache_mla_latent[l], rope_t, w["wuv_pad"], n_pages)
    ob = _moba_sample(page_flat, qm.reshape(bd, dec, -1), km.reshape(bd, dec, -1), vm.reshape(bd, dec, -1),
                      moba_kt, moba_vt, n_pages)
    y_s = _peer(_merge(hs, ya.reshape(n_s, -1), ob.reshape(n_s, -1), w["w_up_b"], w, tm_s), w,
                _token_tile(n_s, 256))
    outs_s = (c.reshape(1, bd, dec, KV_LORA), kr.reshape(1, bd, dec, MLA_ROPE),
              km.reshape(1, bd, dec, MOBA_HEADS, MOBA_DH), vm.reshape(1, bd, dec, MOBA_HEADS, MOBA_DH))
    return (y_p.reshape(batch, seq, D_MODEL), y_s.reshape(bd, dec, D_MODEL), *outs_p, *outs_s)
```

```python
import functools

import numpy as np
import jax
import jax.numpy as jnp
from jax import lax
from jax.experimental import pallas as pl
from jax.experimental.pallas import tpu as pltpu

F32 = jnp.float32
BF16 = jnp.bfloat16

D_MODEL = 1024
PAGE_SIZE = 128

MLA_HEADS = 8
MLA_NOPE = 64
MLA_ROPE = 32
MLA_V = 64
Q_LORA = 768
KV_LORA = 256
ROPE_THETA = 10000.0
MLA_SCALE = (MLA_NOPE + MLA_ROPE) ** -0.5
ROPE_PAD = 128
QCAT = KV_LORA + ROPE_PAD

MOBA_HEADS = 8
MOBA_DH = 64
MOBA_WIDTH = MOBA_HEADS * MOBA_DH
MOBA_BLOCK = 256
MOBA_TOPK = 3
MOBA_SCALE = MOBA_DH ** -0.5
MOBA_PAD = 128
MOBA_PADW = MOBA_HEADS * MOBA_PAD

PEER_HEADS = 8
PEER_NKEYS = 128
PEER_DKEY = 256
PEER_TOPK = 16
PEER_ECHUNK = 4096
PEER_ICHUNK = PEER_ECHUNK // PEER_NKEYS
PEER_KTILE = 256
PEER_NCAND = 72

RMS_EPS = 1e-6
NEG_INF = -1e30

VMEM_LIMIT_BYTES = 56 * 1024 * 1024


def _cparams(*sem):
    return pltpu.CompilerParams(dimension_semantics=sem, vmem_limit_bytes=VMEM_LIMIT_BYTES)


def _rms(xf, w):
    return xf * lax.rsqrt(jnp.mean(xf * xf, axis=-1, keepdims=True) + RMS_EPS) * w


def _dot_nt(a, b):
    return lax.dot_general(a, b, (((1,), (1,)), ((), ())), preferred_element_type=F32)


def _full_spec(shape):
    n = len(shape)
    return pl.BlockSpec(shape, lambda *_: (0,) * n)


_A_QA = 0
_A_CKV = Q_LORA
_A_QM = _A_CKV + KV_LORA
_A_KM = _A_QM + MOBA_WIDTH
_A_VM = _A_KM + MOBA_WIDTH
_A_KR = _A_VM + MOBA_WIDTH
_A_KRS = _A_KR + ROPE_PAD
_A_WIDTH = _A_KRS + ROPE_PAD
_Q_NOPE = 0
_Q_ROPE = MLA_HEADS * MLA_NOPE
_Q_ROPES = _Q_ROPE + MLA_HEADS * ROPE_PAD
_Q_WIDTH = _Q_ROPES + MLA_HEADS * ROPE_PAD


def _pad_heads(v, lane_lo, fill):
    out = []
    for p in range(MOBA_HEADS // 2):
        blk = v[:, p * 128:(p + 1) * 128]
        out.append(jnp.where(lane_lo, blk, fill))
        out.append(jnp.where(lane_lo, pltpu.roll(blk, MOBA_DH, axis=1), fill))
    return jnp.concatenate(out, axis=1)


def _inproj_body(x_ref, cs_ref, sn_ref, an_ref, wa_ref, qan_ref, wqb_ref, kvn_ref, wuk_ref,
                 qcat_ref, kcat_ref, c_ref, kr_ref, qm_ref, km_ref, vm_ref, qp_ref, kp_ref, vp_ref):
    xn = _rms(x_ref[...], an_ref[...]).astype(BF16)
    z = jnp.dot(xn, wa_ref[...], preferred_element_type=F32)
    cs = cs_ref[...]
    sn = sn_ref[...]
    qn = _rms(z[:, _A_QA:_A_QA + Q_LORA], qan_ref[...]).astype(BF16)
    q = jnp.dot(qn, wqb_ref[...], preferred_element_type=F32)
    for p in range(MLA_HEADS // 2):
        lat = jnp.dot(q[:, 128 * p:128 * (p + 1)].astype(BF16), wuk_ref[p], preferred_element_type=F32)
        for hl in range(2):
            h = 2 * p + hl
            qcat_ref[:, h * QCAT:h * QCAT + KV_LORA] = (lat[:, hl * KV_LORA:(hl + 1) * KV_LORA] * MLA_SCALE).astype(BF16)
            rope = (q[:, _Q_ROPE + h * ROPE_PAD:_Q_ROPE + (h + 1) * ROPE_PAD] * cs
                    + q[:, _Q_ROPES + h * ROPE_PAD:_Q_ROPES + (h + 1) * ROPE_PAD] * sn)
            qcat_ref[:, h * QCAT + KV_LORA:(h + 1) * QCAT] = (rope * MLA_SCALE).astype(BF16)
    c = _rms(z[:, _A_CKV:_A_CKV + KV_LORA], kvn_ref[...])
    c_ref[...] = c
    kr = z[:, _A_KR:_A_KR + ROPE_PAD] * cs + z[:, _A_KRS:_A_KRS + ROPE_PAD] * sn
    kr_ref[...] = kr[:, :MLA_ROPE]
    kcat_ref[:, :KV_LORA] = c.astype(BF16)
    ones_lane = lax.broadcasted_iota(jnp.int32, kr.shape, 1) == ROPE_PAD - 1
    kcat_ref[:, KV_LORA:] = jnp.where(ones_lane, 1.0, kr).astype(BF16)
    qm = z[:, _A_QM:_A_QM + MOBA_WIDTH]
    km = z[:, _A_KM:_A_KM + MOBA_WIDTH]
    vm = z[:, _A_VM:_A_VM + MOBA_WIDTH]
    qm_ref[...] = qm
    km_ref[...] = km
    vm_ref[...] = vm
    lane = lax.broadcasted_iota(jnp.int32, (qm.shape[0], 128), 1)
    lane_lo = lane < MOBA_DH
    zeros = jnp.zeros(lane.shape, F32)
    qp_ref[...] = _pad_heads(qm * MOBA_SCALE, lane_lo, zeros).astype(BF16)
    kp_ref[...] = _pad_heads(km, lane_lo, zeros).astype(BF16)
    vp_ref[...] = _pad_heads(vm, lane_lo, jnp.where(lane == MOBA_PAD - 1, 1.0, 0.0)).astype(BF16)


def _inproj(x2, cs, sn, w, tm):
    n = x2.shape[0]
    tab_blocks = cs.shape[0] // tm
    row = lambda i: (i, 0)
    tab = lambda i: (i % tab_blocks, 0)
    widths = [(MLA_HEADS * QCAT, BF16), (QCAT, BF16), (KV_LORA, F32), (MLA_ROPE, F32),
              (MOBA_WIDTH, F32), (MOBA_WIDTH, F32), (MOBA_WIDTH, F32),
              (MOBA_PADW, BF16), (MOBA_PADW, BF16), (MOBA_PADW, BF16)]
    return pl.pallas_call(
        _inproj_body,
        grid=(n // tm,),
        in_specs=[pl.BlockSpec((tm, D_MODEL), row),
                  pl.BlockSpec((tm, ROPE_PAD), tab), pl.BlockSpec((tm, ROPE_PAD), tab),
                  _full_spec((1, D_MODEL)), _full_spec((D_MODEL, _A_WIDTH)),
                  _full_spec((1, Q_LORA)), _full_spec((Q_LORA, _Q_WIDTH)),
                  _full_spec((1, KV_LORA)), _full_spec((MLA_HEADS // 2, 128, 2 * KV_LORA))],
        out_specs=[pl.BlockSpec((tm, wd), row) for wd, _ in widths],
        out_shape=[jax.ShapeDtypeStruct((n, wd), dt) for wd, dt in widths],
        compiler_params=_cparams("arbitrary"),
        name="inproj",
    )(x2, cs, sn, w["attn_norm"], w["w_a"], w["q_a_norm"], w["w_qb"], w["kv_a_norm"], w["w_uk_pairs"])


def _softmax_step(s, v, m_scr, l_scr, acc_scr):
    m_prev = m_scr[...]
    m_new = jnp.maximum(m_prev, jnp.max(s, axis=-1, keepdims=True))
    alpha = jnp.exp(m_prev - m_new)
    p = jnp.exp(s - m_new)
    l_scr[...] = alpha * l_scr[...] + jnp.sum(p, axis=-1, keepdims=True)
    acc_scr[...] = alpha * acc_scr[...] + jnp.dot(p.astype(BF16), v, preferred_element_type=F32)
    m_scr[...] = m_new


def _mla_prompt_body(qcat_ref, kcat_ref, wuv_ref, out_ref, m_scr, acc_scr, *, tq):
    qi = pl.program_id(1)
    m_scr[...] = jnp.full(m_scr.shape, NEG_INF, F32)
    acc_scr[...] = jnp.zeros(acc_scr.shape, F32)

    def chunk(kc, masked):
        k = kcat_ref[pl.ds(pl.multiple_of(kc * tq, tq), tq), :]
        for h in range(MLA_HEADS):
            s = _dot_nt(qcat_ref[:, h * QCAT:(h + 1) * QCAT], k)
            if masked:
                tok = lax.broadcasted_iota(jnp.int32, s.shape, 0)
                col = lax.broadcasted_iota(jnp.int32, s.shape, 1)
                s = jnp.where(col <= tok, s, NEG_INF)
            m_prev = m_scr[h]
            m_new = jnp.maximum(m_prev, jnp.max(s, axis=-1, keepdims=True))
            p = jnp.exp(s - m_new).astype(BF16)
            acc_scr[h] = jnp.exp(m_prev - m_new) * acc_scr[h] + jnp.dot(p, k, preferred_element_type=F32)
            m_scr[h] = m_new

    def past(kc, carry):
        chunk(kc, False)
        return carry

    lax.fori_loop(0, qi, past, 0)
    chunk(qi, True)
    y = jnp.zeros(out_ref.shape, F32)
    for h in range(MLA_HEADS):
        acc = acc_scr[h]
        o = (acc[:, :KV_LORA] / acc[:, QCAT - 1:QCAT]).astype(BF16)
        y = y + jnp.dot(o, wuv_ref[h], preferred_element_type=F32)
    out_ref[...] = y.astype(out_ref.dtype)


def _mla_prompt(qcat, kcat, wuv_pad, batch, seq, tq):
    nq = seq // tq
    return pl.pallas_call(
        functools.partial(_mla_prompt_body, tq=tq),
        grid=(batch, nq),
        in_specs=[pl.BlockSpec((tq, MLA_HEADS * QCAT), lambda b, i: (b * nq + i, 0)),
                  pl.BlockSpec((seq, QCAT), lambda b, i: (b, 0)),
                  _full_spec((MLA_HEADS, KV_LORA, MLA_HEADS * MLA_V))],
        out_specs=pl.BlockSpec((tq, MLA_HEADS * MLA_V), lambda b, i: (b * nq + i, 0)),
        out_shape=jax.ShapeDtypeStruct((batch * seq, MLA_HEADS * MLA_V), BF16),
        scratch_shapes=[pltpu.VMEM((MLA_HEADS, tq, 1), F32), pltpu.VMEM((MLA_HEADS, tq, QCAT), F32)],
        compiler_params=_cparams("arbitrary", "arbitrary"),
        name="mla_prompt",
    )(qcat, kcat, wuv_pad)


def _alibi_slope(h):
    return 2.0 ** (-8.0 * (h + 1) / MOBA_HEADS)


def _block_choice(scores, n_valid):
    lane = lax.broadcasted_iota(jnp.int32, scores.shape, 1)
    n = lane & 7
    s = jnp.where(n < n_valid, scores, NEG_INF)
    beaten = jnp.zeros(scores.shape, F32)
    for d in range(1, 8):
        other = jnp.where(n >= d, pltpu.roll(s, d, axis=1), pltpu.roll(s, 128 - 8 + d, axis=1))
        wins = (other > s) | ((other == s) & (n >= d))
        beaten = beaten + wins.astype(F32)
    return jnp.where((beaten < MOBA_TOPK) & (n < n_valid), 1.0, 0.0)


def _moba_prompt_body(qf_ref, kf_ref, qp_ref, kp_ref, vp_ref, out_ref, kmean_scr, sel_scr, m_scr, acc_scr):
    qi = pl.program_id(1)
    nblk = kmean_scr.shape[0]
    blk = MOBA_BLOCK

    @pl.when(qi == 0)
    def _():
        kmean_scr[...] = jnp.zeros(kmean_scr.shape, F32)
        for n in range(kf_ref.shape[0] // blk):
            kmean_scr[n:n + 1, :] = jnp.mean(kf_ref[n * blk:(n + 1) * blk, :], axis=0, keepdims=True)

    m_scr[...] = jnp.full(m_scr.shape, NEG_INF, F32)
    acc_scr[...] = jnp.zeros(acc_scr.shape, F32)
    km = kmean_scr[...]
    tiled = jnp.concatenate([km] * MOBA_HEADS + [jnp.zeros((128 - MOBA_HEADS * nblk, MOBA_WIDTH), F32)], axis=0)
    row_h = lax.broadcasted_iota(jnp.int32, tiled.shape, 0) // nblk
    lane_h = lax.broadcasted_iota(jnp.int32, tiled.shape, 1) // MOBA_DH
    kmbd = jnp.where(row_h == lane_h, tiled, 0.0)
    scores = lax.dot_general(qf_ref[...], kmbd, (((1,), (1,)), ((), ())),
                             preferred_element_type=F32, precision=lax.Precision.HIGHEST)
    sel_scr[...] = _block_choice(scores, qi).astype(BF16)

    def attend(j, own):
        rows = pl.ds(pl.multiple_of(j * blk, blk), blk)
        r = lax.broadcasted_iota(jnp.int32, (blk, blk), 0)
        c = lax.broadcasted_iota(jnp.int32, (blk, blk), 1)
        dist = ((qi - j) * blk + (r - c)).astype(F32)
        if not own:
            er = lax.broadcasted_iota(jnp.int32, (128, MOBA_PADW), 0)
            ec = lax.broadcasted_iota(jnp.int32, (128, MOBA_PADW), 1)
            spread = jnp.where(((er >> 3) == (ec >> 7)) & ((er & 7) == j), 1.0, 0.0).astype(BF16)
            chosen = jnp.dot(sel_scr[...], spread, preferred_element_type=F32)
        for h in range(MOBA_HEADS):
            sl = slice(h * MOBA_PAD, (h + 1) * MOBA_PAD)
            s = _dot_nt(qp_ref[:, sl], kp_ref[rows, sl])
            s = s - _alibi_slope(h) * dist
            if own:
                s = jnp.where(r >= c, s, NEG_INF)
            else:
                s = jnp.where(jnp.concatenate([chosen[:, sl]] * (blk // MOBA_PAD), axis=1) > 0.5, s, NEG_INF)
            m_prev = m_scr[h]
            m_new = jnp.maximum(m_prev, jnp.max(s, axis=-1, keepdims=True))
            p = jnp.exp(s - m_new).astype(BF16)
            acc_scr[h] = (jnp.exp(m_prev - m_new) * acc_scr[h]
                          + jnp.dot(p, vp_ref[rows, sl], preferred_element_type=F32))
            m_scr[h] = m_new

    def past(j, carry):
        attend(j, False)
        return carry

    lax.fori_loop(0, qi, past, 0)
    attend(qi, True)
    head_lane = lax.broadcasted_iota(jnp.int32, (blk, MOBA_PAD), 1) < MOBA_DH
    for h in range(MOBA_HEADS):
        acc = acc_scr[h]
        o = jnp.where(head_lane, acc / acc[:, MOBA_PAD - 1:MOBA_PAD], 0.0)
        out_ref[:, h * MOBA_PAD:(h + 1) * MOBA_PAD] = o.astype(out_ref.dtype)


def _moba_prompt(qm, km, qp, kp, vp, batch, seq):
    nb = seq // MOBA_BLOCK
    assert nb <= 8 and seq % MOBA_BLOCK == 0
    qmap = lambda b, i: (b * nb + i, 0)
    bmap = lambda b, i: (b, 0)
    blk = MOBA_BLOCK
    return pl.pallas_call(
        _moba_prompt_body,
        grid=(batch, nb),
        in_specs=[pl.BlockSpec((blk, MOBA_WIDTH), qmap), pl.BlockSpec((seq, MOBA_WIDTH), bmap),
                  pl.BlockSpec((blk, MOBA_PADW), qmap),
                  pl.BlockSpec((seq, MOBA_PADW), bmap), pl.BlockSpec((seq, MOBA_PADW), bmap)],
        out_specs=pl.BlockSpec((blk, MOBA_PADW), qmap),
        out_shape=jax.ShapeDtypeStruct((batch * seq, MOBA_PADW), BF16),
        scratch_shapes=[pltpu.VMEM((8, MOBA_WIDTH), F32), pltpu.VMEM((blk, 128), BF16),
                        pltpu.VMEM((MOBA_HEADS, blk, 1), F32), pltpu.VMEM((MOBA_HEADS, blk, MOBA_PAD), F32)],
        compiler_params=_cparams("arbitrary", "arbitrary"),
        name="moba_prompt",
    )(qm, km, qp, kp, vp)


def _sigmoid(g):
    return 1.0 / (1.0 + jnp.exp(-g))


def _merge_body(x_ref, ya_ref, ob_ref, an_ref, wg_ref, wua_ref, wub_ref, wo_ref, h_ref):
    x = x_ref[...]
    xn = _rms(x, an_ref[...]).astype(BF16)
    g = jnp.dot(xn, wg_ref[...], preferred_element_type=F32)
    y_a = jnp.dot(ya_ref[...], wua_ref[...], preferred_element_type=F32)
    y_b = jnp.dot(ob_ref[...], wub_ref[...], preferred_element_type=F32)
    mix = _sigmoid(g[:, :D_MODEL]) * y_a + _sigmoid(g[:, D_MODEL:]) * y_b
    h_ref[...] = x + jnp.dot(mix.astype(BF16), wo_ref[...], preferred_element_type=F32)


def _merge(x2, ya, ob, w_up_b, w, tm):
    n = x2.shape[0]
    wb = ob.shape[1]
    row = lambda i: (i, 0)
    return pl.pallas_call(
        _merge_body,
        grid=(n // tm,),
        in_specs=[pl.BlockSpec((tm, D_MODEL), row), pl.BlockSpec((tm, MLA_HEADS * MLA_V), row),
                  pl.BlockSpec((tm, wb), row),
                  _full_spec((1, D_MODEL)), _full_spec((D_MODEL, 2 * D_MODEL)),
                  _full_spec((MLA_HEADS * MLA_V, D_MODEL)), _full_spec((wb, D_MODEL)),
                  _full_spec((D_MODEL, D_MODEL))],
        out_specs=pl.BlockSpec((tm, D_MODEL), row),
        out_shape=jax.ShapeDtypeStruct((n, D_MODEL), F32),
        compiler_params=_cparams("arbitrary"),
        name="merge",
    )(x2, ya, ob, w["attn_norm"], w["w_g"], w["w_up_a"], w_up_b, w["w_o"])


def _topk_extract(s, key):
    rank = jnp.full(s.shape, float(PEER_TOPK), F32)
    work = s
    vals = []
    for r in range(PEER_TOPK):
        m = jnp.max(work, axis=0, keepdims=True)
        if key is None:
            hit = work == m
        else:
            hit = key == jnp.min(jnp.where(work == m, key, 1e9), axis=0, keepdims=True)
        rank = jnp.where(hit, float(r), rank)
        work = jnp.where(hit, -jnp.inf, work)
        vals.append(m)
    return rank, vals


def _topk_ranks(s, key, rank_scr, vals_scr):
    rank, vals = _topk_extract(s, None)
    rank_scr[...] = rank
    vals_scr[...] = jnp.concatenate(vals, axis=0)
    marked = jnp.sum(jnp.where(rank < float(PEER_TOPK), 1.0, 0.0))

    @pl.when(marked != float(PEER_TOPK * s.shape[1]))
    def _():
        rank_k, vals_k = _topk_extract(s, key)
        rank_scr[...] = rank_k
        vals_scr[...] = jnp.concatenate(vals_k, axis=0)

    return rank_scr[...], [vals_scr[r:r + 1, :] for r in range(PEER_TOPK)]


def _gelu(x):
    return 0.5 * x * (1.0 + lax.erf(x * np.float32(np.sqrt(0.5))))


def _peer_body(h_ref, fn_ref, wq_ref, k1_ref, k2_ref, u_ref, vt_ref, on_ref, y_ref,
               xn_scr, qt_scr, r2_scr, e2_scr, nb_scr, p1_scr, acc_scr,
               rk1_scr, rk2_scr, rkc_scr, v1_scr, v2_scr, vc_scr):
    c = pl.program_id(1)
    t = h_ref.shape[0]
    kk = PEER_TOPK

    @pl.when(c == 0)
    def _():
        xn = _rms(h_ref[...], fn_ref[...]).astype(BF16)
        xn_scr[...] = xn
        qt_scr[...] = _dot_nt(wq_ref[...], xn)
        acc_scr[...] = jnp.zeros(acc_scr.shape, F32)
        key128 = lax.broadcasted_iota(jnp.int32, (PEER_NKEYS, t), 0).astype(F32)
        r16 = lax.broadcasted_iota(jnp.int32, (kk, t), 0).astype(F32)
        r8 = lax.broadcasted_iota(jnp.int32, (8, t), 0).astype(F32)

        def per_head(hd, carry):
            base = pl.multiple_of(hd * PEER_DKEY, PEER_DKEY)
            q1 = qt_scr[pl.ds(base, PEER_DKEY // 2), :].astype(BF16)
            q2 = qt_scr[pl.ds(base + PEER_DKEY // 2, PEER_DKEY // 2), :].astype(BF16)
            s1 = jnp.dot(k1_ref[hd], q1, preferred_element_type=F32)
            s2 = jnp.dot(k2_ref[hd], q2, preferred_element_type=F32)
            rank1, v1r = _topk_ranks(s1, key128, rk1_scr, v1_scr)
            rank2, v2r = _topk_ranks(s2, key128, rk2_scr, v2_scr)
            v1, v1h = jnp.concatenate(v1r, axis=0), jnp.concatenate(v1r[:8], axis=0)
            v2, v2h = jnp.concatenate(v2r, axis=0), jnp.concatenate(v2r[:8], axis=0)
            slabs = [(v1 + v2r[0], r16 * kk, None),
                     (v1h + v2r[1], r8 * kk + 1.0, None),
                     (v2 + v1r[0], r16, r16 >= 2.0),
                     (v2h + v1r[1], r8 + float(kk), r8 >= 2.0)]
            for b in (2, 3, 4):
                slabs.append((v1h + v2r[b], r8 * kk + float(b), (r8 >= 2.0) & ((r8 + 1.0) * (b + 1) <= kk)))
            cand = jnp.concatenate([vals if ok is None else jnp.where(ok, vals, -jnp.inf) for vals, _, ok in slabs],
                                   axis=0)
            ckey = jnp.concatenate([key if ok is None else jnp.where(ok, key, 1e6) for _, key, ok in slabs], axis=0)
            rankc, _ = _topk_ranks(cand, ckey, rkc_scr, vc_scr)
            selc = jnp.where(rankc < float(kk), 1.0, 0.0)
            sel_a, sel_b, sel_c, sel_d = selc[0:16], selc[16:24], selc[24:40], selc[40:48]
            sel_e = [selc[48:56], selc[56:64], selc[64:72]]
            e1, e1h = jnp.exp(v1 - v1r[0]), jnp.exp(v1h - v1r[0])
            e2, e2h = jnp.exp(v2 - v2r[0]), jnp.exp(v2h - v2r[0])
            e1r = [jnp.exp(v1r[a] - v1r[0]) for a in range(kk)]
            e2r = [jnp.exp(v2r[b] - v2r[0]) for b in range(kk)]
            colsum = lambda x: jnp.sum(x, axis=0, keepdims=True)
            zero8 = jnp.zeros((8, t), F32)
            nb = (sel_a + jnp.concatenate([sel_b + sel_e[0] + sel_e[1] + sel_e[2], zero8], axis=0)
                  + jnp.where(r16 == 0.0, colsum(sel_c), 0.0) + jnp.where(r16 == 1.0, colsum(sel_d), 0.0))
            zsum = (colsum(sel_a * e1) * e2r[0] + colsum(sel_b * e1h) * e2r[1]
                    + colsum(sel_c * e2) * e1r[0] + colsum(sel_d * e2h) * e1r[1])
            for n, b in enumerate((2, 3, 4)):
                zsum = zsum + colsum(sel_e[n] * e1h) * e2r[b]
            p1x = jnp.where(rank1 < float(kk), jnp.exp(s1 - v1r[0]), 0.0)
            e2x = jnp.where(rank2 < float(kk), jnp.exp(s2 - v2r[0]) / zsum, 0.0)
            nbx = jnp.zeros((PEER_NKEYS, t), F32)
            for a in range(kk):
                nbx = jnp.where(rank1 == float(a), nb[a:a + 1], nbx)
            r2_scr[hd] = rank2.astype(BF16)
            e2_scr[hd] = e2x.astype(BF16)
            nb_scr[hd] = nbx
            p1_scr[hd] = p1x
            return carry

        lax.fori_loop(0, PEER_HEADS, per_head, 0)

    xn = xn_scr[...]
    acc = None
    for kt in range(PEER_ECHUNK // PEER_KTILE):
        rows = slice(kt * PEER_KTILE, (kt + 1) * PEER_KTILE)
        act = _dot_nt(u_ref[rows, :], xn)
        parts = []
        for ii in range(PEER_KTILE // PEER_NKEYS):
            i = c * PEER_ICHUNK + kt * (PEER_KTILE // PEER_NKEYS) + ii
            wgt = jnp.zeros((PEER_NKEYS, t), BF16)
            for hd in range(PEER_HEADS):
                nbrow = nb_scr[hd, pl.ds(i, 1), :].astype(BF16)
                p1row = p1_scr[hd, pl.ds(i, 1), :].astype(BF16)
                wgt = wgt + jnp.where(r2_scr[hd] < nbrow, e2_scr[hd] * p1row, jnp.zeros((), BF16))
            parts.append(wgt * _gelu(act[ii * PEER_NKEYS:(ii + 1) * PEER_NKEYS]).astype(BF16))
        piece = jnp.dot(vt_ref[:, rows], jnp.concatenate(parts, axis=0), preferred_element_type=F32)
        acc = piece if acc is None else acc + piece
    acc_scr[...] += acc

    @pl.when(c == pl.num_programs(1) - 1)
    def _():
        h2 = h_ref[...] + acc_scr[...].T
        y_ref[...] = _rms(h2, on_ref[...])


def _peer(h2d, w, t):
    n = h2d.shape[0]
    n_exp = w["peer_u"].shape[0]
    nchunk = n_exp // PEER_ECHUNK
    slab = (PEER_HEADS, PEER_NKEYS, t)
    return pl.pallas_call(
        _peer_body,
        grid=(n // t, nchunk),
        in_specs=[pl.BlockSpec((t, D_MODEL), lambda i, c: (i, 0)),
                  _full_spec((1, D_MODEL)), _full_spec((PEER_HEADS * PEER_DKEY, D_MODEL)),
                  _full_spec((PEER_HEADS, PEER_NKEYS, PEER_DKEY // 2)),
                  _full_spec((PEER_HEADS, PEER_NKEYS, PEER_DKEY // 2)),
                  pl.BlockSpec((PEER_ECHUNK, D_MODEL), lambda i, c: (c, 0)),
                  pl.BlockSpec((D_MODEL, PEER_ECHUNK), lambda i, c: (0, c)),
                  _full_spec((1, D_MODEL))],
        out_specs=pl.BlockSpec((t, D_MODEL), lambda i, c: (i, 0)),
        out_shape=jax.ShapeDtypeStruct((n, D_MODEL), F32),
        scratch_shapes=[pltpu.VMEM((t, D_MODEL), BF16), pltpu.VMEM((PEER_HEADS * PEER_DKEY, t), F32),
                        pltpu.VMEM(slab, BF16), pltpu.VMEM(slab, BF16), pltpu.VMEM(slab, F32), pltpu.VMEM(slab, F32),
                        pltpu.VMEM((D_MODEL, t), F32),
                        pltpu.VMEM((PEER_NKEYS, t), F32), pltpu.VMEM((PEER_NKEYS, t), F32),
                        pltpu.VMEM((PEER_NCAND, t), F32)] + [pltpu.VMEM((PEER_TOPK, t), F32)] * 3,
        compiler_params=_cparams("arbitrary", "arbitrary"),
        name="peer",
    )(h2d, w["ffn_norm"], w["peer_wq_t"], w["peer_k1"], w["peer_k2"], w["peer_u"], w["peer_vt"], w["final_norm"])


SAMPLE_PAGES_PER_STEP = 16
MLA_PAGES_PER_STEP = 32


def _mla_sample_body(pt_ref, q_ref, knew_ref, *rest, pages, dec):
    c_refs = rest[:pages]
    r_refs = rest[pages:2 * pages]
    wuv_ref, out_ref, q_scr, knew_scr, m_scr, l_scr, acc_scr = rest[2 * pages:]
    s_id = pl.program_id(1)
    rows = MLA_HEADS * dec

    @pl.when(s_id == 0)
    def _():
        for h in range(MLA_HEADS):
            q_scr[h * dec:(h + 1) * dec, :] = q_ref[0, :, h * QCAT:(h + 1) * QCAT].astype(F32)
        m_scr[...] = jnp.full(m_scr.shape, NEG_INF, F32)
        l_scr[...] = jnp.zeros(l_scr.shape, F32)
        acc_scr[...] = jnp.zeros(acc_scr.shape, F32)

    q = q_scr[...].astype(BF16)
    q_lat = q[:, :KV_LORA]
    q_rope = q[:, KV_LORA:KV_LORA + MLA_ROPE]
    cs = [c_refs[i][0].astype(BF16) for i in range(pages)]
    s = jnp.concatenate([_dot_nt(q_lat, cs[i])
                         + jnp.dot(q_rope, r_refs[i][0].astype(BF16), preferred_element_type=F32)
                         for i in range(pages)], axis=1)
    m_prev = m_scr[...]
    m_new = jnp.maximum(m_prev, jnp.max(s, axis=-1, keepdims=True))
    alpha = jnp.exp(m_prev - m_new)
    p = jnp.exp(s - m_new)
    l_scr[...] = alpha * l_scr[...] + jnp.sum(p, axis=-1, keepdims=True)
    p = p.astype(BF16)
    pv = jnp.zeros(acc_scr.shape, F32)
    for i in range(pages):
        pv = pv + jnp.dot(p[:, i * PAGE_SIZE:(i + 1) * PAGE_SIZE], cs[i], preferred_element_type=F32)
    acc_scr[...] = alpha * acc_scr[...] + pv
    m_scr[...] = m_new

    @pl.when(s_id == pl.num_programs(1) - 1)
    def _():
        knew_scr[...] = jnp.zeros(knew_scr.shape, F32)
        knew_scr[0:dec, :] = knew_ref[0].astype(F32)
        k = knew_scr[...].astype(BF16)
        s_new = _dot_nt(q, k)
        tok = lax.broadcasted_iota(jnp.int32, s_new.shape, 0) & (dec - 1)
        col = lax.broadcasted_iota(jnp.int32, s_new.shape, 1)
        s_new = jnp.where(col <= tok, s_new, NEG_INF)
        _softmax_step(s_new, k[:, :KV_LORA], m_scr, l_scr, acc_scr)
        o = (acc_scr[...] / l_scr[...]).astype(BF16)
        y = jnp.zeros((dec, MLA_HEADS * MLA_V), F32)
        for h in range(MLA_HEADS):
            y = y + jnp.dot(o[h * dec:(h + 1) * dec], wuv_ref[h], preferred_element_type=F32)
        out_ref[0] = y.astype(out_ref.dtype)


def _mla_sample(page_flat, qcat3, kcat3, cache_c, cache_r, wuv_pad, n_pages):
    bd, dec, _ = qcat3.shape
    assert dec & (dec - 1) == 0 and dec <= PAGE_SIZE
    pages = min(MLA_PAGES_PER_STEP, n_pages)
    steps = n_pages // pages
    rows = MLA_HEADS * dec

    def page_map(i):
        return lambda b, s, pt: (pt[b * n_pages + s * pages + i], 0, 0)

    per_b = lambda b, s, pt: (b, 0, 0)
    grid_spec = pltpu.PrefetchScalarGridSpec(
        num_scalar_prefetch=1,
        grid=(bd, steps),
        in_specs=([pl.BlockSpec((1, dec, MLA_HEADS * QCAT), per_b), pl.BlockSpec((1, dec, QCAT), per_b)]
                  + [pl.BlockSpec((1, PAGE_SIZE, KV_LORA), page_map(i)) for i in range(pages)]
                  + [pl.BlockSpec((1, MLA_ROPE, PAGE_SIZE), page_map(i)) for i in range(pages)]
                  + [pl.BlockSpec((MLA_HEADS, KV_LORA, MLA_HEADS * MLA_V), lambda b, s, pt: (0, 0, 0))]),
        out_specs=pl.BlockSpec((1, dec, MLA_HEADS * MLA_V), per_b),
        scratch_shapes=[pltpu.VMEM((rows, QCAT), F32), pltpu.VMEM((PAGE_SIZE, QCAT), F32),
                        pltpu.VMEM((rows, 1), F32), pltpu.VMEM((rows, 1), F32), pltpu.VMEM((rows, KV_LORA), F32)],
    )
    return pl.pallas_call(
        functools.partial(_mla_sample_body, pages=pages, dec=dec),
        grid_spec=grid_spec,
        out_shape=jax.ShapeDtypeStruct((bd, dec, MLA_HEADS * MLA_V), BF16),
        compiler_params=_cparams("arbitrary", "arbitrary"),
        name="mla_sample",
    )(page_flat, qcat3, kcat3, *([cache_c] * pages), *([cache_r] * pages), wuv_pad)


def _group_choice(scores, group, n_valid, topk):
    lane = lax.broadcasted_iota(jnp.int32, scores.shape, 1)
    s = jnp.where(lane < n_valid, scores, NEG_INF)
    beaten = jnp.zeros(scores.shape, F32)
    for d in range(1, group):
        other = jnp.where(lane >= d, pltpu.roll(s, d, axis=1), pltpu.roll(s, 128 - group + d, axis=1))
        wins = (other > s) | ((other == s) & (lane >= d))
        beaten = beaten + wins.astype(F32)
    return jnp.where((beaten < topk) & (lane < n_valid), 1.0, 0.0)


def _moba_sample_body(pt_ref, qm_ref, kn_ref, vn_ref, e_ref, *rest, pages, dec, n_pages):
    k_refs = rest[:pages]
    v_refs = rest[pages:2 * pages]
    out_ref, qbd_scr, kmt_scr, s_scr, snew_scr, new_scr, l_scr, acc_scr = rest[2 * pages:]
    s_id = pl.program_id(1)
    n_k = n_pages // pages
    rows = MOBA_HEADS * dec
    n_blocks = n_pages // 2
    past = n_pages * PAGE_SIZE
    row = lax.broadcasted_iota(jnp.int32, (rows, PAGE_SIZE), 0)
    col = lax.broadcasted_iota(jnp.int32, (rows, PAGE_SIZE), 1)
    tok = row & (dec - 1)
    slope = jnp.zeros((rows, PAGE_SIZE), F32)
    for h in range(MOBA_HEADS):
        slope = jnp.where(row // dec == h, _alibi_slope(h), slope)

    @pl.when(s_id == 0)
    def _():
        tiled = jnp.concatenate([qm_ref[0]] * MOBA_HEADS, axis=0)
        row_h = lax.broadcasted_iota(jnp.int32, tiled.shape, 0) // dec
        lane_h = lax.broadcasted_iota(jnp.int32, tiled.shape, 1) // MOBA_DH
        qbd_scr[...] = jnp.where(row_h == lane_h, tiled * MOBA_SCALE, 0.0)
        kmt_scr[...] = jnp.zeros(kmt_scr.shape, F32)

    @pl.when(s_id < n_k)
    def _():
        qbd = qbd_scr[...].astype(BF16)
        lane = lax.broadcasted_iota(jnp.int32, kmt_scr.shape, 1)
        for i in range(0, pages, 2):
            kt0 = k_refs[i][0]
            kt1 = k_refs[i + 1][0]
            page = s_id * pages + i
            s_scr[page] = jnp.dot(qbd, kt0.astype(BF16), preferred_element_type=F32)
            s_scr[page + 1] = jnp.dot(qbd, kt1.astype(BF16), preferred_element_type=F32)
            mean = 0.5 * jnp.mean(kt0 + kt1, axis=1, keepdims=True)
            kmt_scr[...] = jnp.where(lane == page // 2, mean, kmt_scr[...])

    @pl.when(s_id == n_k)
    def _():
        scores = jnp.dot(qbd_scr[...], kmt_scr[...], preferred_element_type=F32, precision=lax.Precision.HIGHEST)
        sel = _group_choice(scores, max(n_blocks, 1), n_blocks, MOBA_TOPK)
        chosen = jnp.dot(sel.astype(BF16), e_ref[...], preferred_element_type=F32)
        new_scr[...] = jnp.zeros(new_scr.shape, F32)
        new_scr[0:dec, :] = kn_ref[0]
        s_new = _dot_nt(qbd_scr[...].astype(BF16), new_scr[...].astype(BF16))
        s_new = s_new - slope * (tok - col).astype(F32)
        s_new = jnp.where(col <= tok, s_new, NEG_INF)
        bias = slope * (past + tok - col).astype(F32)
        mx = s_new
        for page in range(n_pages):
            cols = slice(page * PAGE_SIZE, (page + 1) * PAGE_SIZE)
            sm = jnp.where(chosen[:, cols] > 0.5, s_scr[page] - (bias - slope * float(page * PAGE_SIZE)), NEG_INF)
            s_scr[page] = sm
            mx = jnp.maximum(mx, sm)
        m = jnp.max(mx, axis=-1, keepdims=True)
        p_new = jnp.exp(s_new - m)
        snew_scr[...] = p_new
        lsum = p_new
        for page in range(n_pages):
            p = jnp.exp(s_scr[page] - m)
            s_scr[page] = p
            lsum = lsum + p
        l_scr[...] = jnp.sum(lsum, axis=-1, keepdims=True)
        acc_scr[...] = jnp.zeros(acc_scr.shape, F32)

    @pl.when(s_id >= n_k)
    def _():
        pv = jnp.zeros(acc_scr.shape, F32)
        for i in range(pages):
            page = (s_id - n_k) * pages + i
            pv = pv + _dot_nt(s_scr[page].astype(BF16), v_refs[i][0].astype(BF16))
        acc_scr[...] += pv

    @pl.when(s_id == 2 * n_k - 1)
    def _():
        new_scr[...] = jnp.zeros(new_scr.shape, F32)
        new_scr[0:dec, :] = vn_ref[0]
        acc = acc_scr[...] + jnp.dot(snew_scr[...].astype(BF16), new_scr[...].astype(BF16),
                                     preferred_element_type=F32)
        o = acc / l_scr[...]
        row_h = lax.broadcasted_iota(jnp.int32, o.shape, 0) // dec
        lane_h = lax.broadcasted_iota(jnp.int32, o.shape, 1) // MOBA_DH
        o = jnp.where(row_h == lane_h, o, 0.0)
        y = o[0:dec]
        for h in range(1, MOBA_HEADS):
            y = y + o[h * dec:(h + 1) * dec]
        out_ref[0] = y.astype(out_ref.dtype)


def _moba_sample(page_flat, qm3, km3, vm3, cache_kt, cache_vt, n_pages):
    bd, dec, _ = qm3.shape
    assert dec == 8, "row groups of the stacked heads must be one sublane tile"
    assert n_pages % 2 == 0, "the past must be a whole number of MoBA blocks"
    n_blocks = n_pages // 2
    assert n_blocks & (n_blocks - 1) == 0 and n_blocks <= 128
    pages = min(SAMPLE_PAGES_PER_STEP, n_pages)
    n_k = n_pages // pages
    rows = MOBA_HEADS * dec

    def k_map(i):
        return lambda b, s, pt: (pt[b * n_pages + jnp.minimum(s, n_k - 1) * pages + i], 0, 0)

    def v_map(i):
        return lambda b, s, pt: (pt[b * n_pages + jnp.maximum(s - n_k, 0) * pages + i], 0, 0)

    per_b = lambda b, s, pt: (b, 0, 0)
    n_keys = n_pages * PAGE_SIZE
    member = (jnp.arange(n_keys)[None, :] // MOBA_BLOCK == jnp.arange(128)[:, None]).astype(BF16)
    grid_spec = pltpu.PrefetchScalarGridSpec(
        num_scalar_prefetch=1,
        grid=(bd, 2 * n_k),
        in_specs=([pl.BlockSpec((1, dec, MOBA_WIDTH), per_b)] * 3
                  + [pl.BlockSpec((128, n_keys), lambda b, s, pt: (0, 0))]
                  + [pl.BlockSpec((1, MOBA_WIDTH, PAGE_SIZE), k_map(i)) for i in range(pages)]
                  + [pl.BlockSpec((1, MOBA_WIDTH, PAGE_SIZE), v_map(i)) for i in range(pages)]),
        out_specs=pl.BlockSpec((1, dec, MOBA_WIDTH), per_b),
        scratch_shapes=[pltpu.VMEM((rows, MOBA_WIDTH), F32), pltpu.VMEM((MOBA_WIDTH, 128), F32),
                        pltpu.VMEM((n_pages, rows, PAGE_SIZE), F32), pltpu.VMEM((rows, PAGE_SIZE), F32),
                        pltpu.VMEM((PAGE_SIZE, MOBA_WIDTH), F32),
                        pltpu.VMEM((rows, 1), F32), pltpu.VMEM((rows, MOBA_WIDTH), F32)],
    )
    return pl.pallas_call(
        functools.partial(_moba_sample_body, pages=pages, dec=dec, n_pages=n_pages),
        grid_spec=grid_spec,
        out_shape=jax.ShapeDtypeStruct((bd, dec, MOBA_WIDTH), BF16),
        compiler_params=_cparams("arbitrary", "arbitrary"),
        name="moba_sample",
    )(page_flat, qm3, km3, vm3, member, *([cache_kt] * pages), *([cache_vt] * pages))


def _rope_tables(pos):
    half = MLA_ROPE // 2
    inv = ROPE_THETA ** (-jnp.arange(half, dtype=F32) / half)
    ang = pos.astype(F32)[:, None] * inv
    cos, sin = jnp.cos(ang), jnp.sin(ang)
    pad = jnp.zeros((pos.shape[0], ROPE_PAD - MLA_ROPE), F32)
    return jnp.concatenate([cos, cos, pad], axis=1), jnp.concatenate([-sin, sin, pad], axis=1)


def _swap_halves(w):
    half = MLA_ROPE // 2
    return jnp.concatenate([w[..., half:], w[..., :half]], axis=-1)


def _layer_weights(l, attn_norm, w_in, q_a_norm, w_q_b, kv_a_norm, w_uk, w_uv, w_up_a, w_up_b, w_o,
                   ffn_norm, peer_w_query, peer_keys_1, peer_keys_2, peer_u, peer_v, final_norm):
    sizes = (Q_LORA, KV_LORA, MLA_ROPE, MOBA_WIDTH, MOBA_WIDTH, MOBA_WIDTH, D_MODEL, D_MODEL)
    qa, ckv, kr, qm, km, vm, ga, gb = jnp.split(w_in[l], [int(v) for v in np.cumsum(sizes)[:-1]], axis=1)
    pad = jnp.zeros((D_MODEL, ROPE_PAD - MLA_ROPE), F32)
    w_a = jnp.concatenate([qa, ckv, qm, km, vm, kr, pad, _swap_halves(kr), pad], axis=1)
    wq = w_q_b[l].reshape(Q_LORA, MLA_HEADS, MLA_NOPE + MLA_ROPE)
    nope = wq[:, :, :MLA_NOPE].reshape(Q_LORA, MLA_HEADS * MLA_NOPE)
    rope = wq[:, :, MLA_NOPE:]
    padq = ((0, 0), (0, 0), (0, ROPE_PAD - MLA_ROPE))
    rope_p = jnp.pad(rope, padq).reshape(Q_LORA, MLA_HEADS * ROPE_PAD)
    ropes_p = jnp.pad(_swap_halves(rope), padq).reshape(Q_LORA, MLA_HEADS * ROPE_PAD)
    w_qb = jnp.concatenate([nope, rope_p, ropes_p], axis=1)
    wuk_t = jnp.transpose(w_uk[l], (1, 2, 0))
    z = jnp.zeros((MLA_NOPE, KV_LORA), F32)
    pairs = [jnp.concatenate([jnp.concatenate([wuk_t[2 * p], z], axis=1),
                              jnp.concatenate([z, wuk_t[2 * p + 1]], axis=1)], axis=0)
             for p in range(MLA_HEADS // 2)]
    wuv_t = jnp.transpose(w_uv[l], (1, 0, 2))
    wuv_pad = jnp.stack([jnp.pad(wuv_t[h], ((0, 0), (h * MLA_V, (MLA_HEADS - 1 - h) * MLA_V)))
                         for h in range(MLA_HEADS)])
    wub = w_up_b[l].reshape(MOBA_HEADS, MOBA_DH, D_MODEL)
    wub_pad = jnp.pad(wub, ((0, 0), (0, MOBA_PAD - MOBA_DH), (0, 0))).reshape(MOBA_PADW, D_MODEL)
    return {
        "attn_norm": attn_norm[l][None, :], "w_a": w_a.astype(BF16),
        "q_a_norm": q_a_norm[l][None, :], "w_qb": w_qb.astype(BF16),
        "kv_a_norm": kv_a_norm[l][None, :], "w_uk_pairs": jnp.stack(pairs).astype(BF16),
        "wuv_pad": wuv_pad.astype(BF16),
        "w_g": jnp.concatenate([ga, gb], axis=1).astype(BF16),
        "w_up_a": w_up_a[l].astype(BF16), "w_up_b": w_up_b[l].astype(BF16), "w_up_b_pad": wub_pad.astype(BF16),
        "w_o": w_o[l].astype(BF16),
        "ffn_norm": ffn_norm[l][None, :], "peer_wq_t": peer_w_query[l].T.astype(BF16),
        "peer_k1": peer_keys_1[l].astype(BF16), "peer_k2": peer_keys_2[l].astype(BF16),
        "peer_u": peer_u[l].astype(BF16), "peer_vt": peer_v[l].T.astype(BF16),
        "final_norm": final_norm[None, :],
    }


def _token_tile(n, want):
    t = min(want, n)
    assert n % t == 0
    return t


def kernel(x_prompt, x_sample, cache_mla_latent, cache_mla_rope, cache_moba_k, cache_moba_v, page_table,
           attn_norm, w_in, q_a_norm, w_q_b, kv_a_norm, w_uk, w_uv, w_up_a, w_up_b, w_o,
           ffn_norm, peer_w_query, peer_keys_1, peer_keys_2, peer_u, peer_v, final_norm):
    batch, seq, _ = x_prompt.shape
    bd, dec, _ = x_sample.shape
    depth = w_in.shape[0]
    n_pages = page_table.shape[1]
    past = n_pages * PAGE_SIZE
    n_p, n_s = batch * seq, bd * dec
    tm_p = _token_tile(seq, 256)
    tm_s = _token_tile(n_s, 256)
    assert tm_s % dec == 0
    cs_p, sn_p = _rope_tables(jnp.arange(seq))
    cs_s, sn_s = _rope_tables(past + jnp.arange(dec))
    cs_s, sn_s = jnp.tile(cs_s, (tm_s // dec, 1)), jnp.tile(sn_s, (tm_s // dec, 1))
    page_flat = page_table.reshape(-1).astype(jnp.int32)

    assert depth == 1, "the PEER kernel applies the final norm, so it closes the only layer"
    l = 0
    w = _layer_weights(l, attn_norm, w_in, q_a_norm, w_q_b, kv_a_norm, w_uk, w_uv, w_up_a, w_up_b, w_o,
                       ffn_norm, peer_w_query, peer_keys_1, peer_keys_2, peer_u, peer_v, final_norm)
    hp = x_prompt.reshape(n_p, D_MODEL)
    qcat, kcat, c, kr, qm, km, vm, qp, kp, vp = _inproj(hp, cs_p, sn_p, w, tm_p)
    ya = _mla_prompt(qcat, kcat, w["wuv_pad"], batch, seq, _token_tile(seq, 256))
    ob = _moba_prompt(qm, km, qp, kp, vp, batch, seq)
    y_p = _peer(_merge(hp, ya, ob, w["w_up_b_pad"], w, tm_p), w, _token_tile(n_p, 256))
    outs_p = (c.reshape(1, batch, seq, KV_LORA), kr.reshape(1, batch, seq, MLA_ROPE),
              km.reshape(1, batch, seq, MOBA_HEADS, MOBA_DH), vm.reshape(1, batch, seq, MOBA_HEADS, MOBA_DH))
    hs = x_sample.reshape(n_s, D_MODEL)
    qcat, kcat, c, kr, qm, km, vm, _, _, _ = _inproj(hs, cs_s, sn_s, w, tm_s)
    rope_t = jnp.swapaxes(cache_mla_rope[l], 1, 2)
    moba_kt = jnp.transpose(cache_moba_k[l], (0, 2, 3, 1)).reshape(-1, MOBA_WIDTH, PAGE_SIZE)
    moba_vt = jnp.transpose(cache_moba_v[l], (0, 2, 3, 1)).reshape(-1, MOBA_WIDTH, PAGE_SIZE)
    ya = _mla_sample(page_flat, qcat.reshape(bd, dec, -1), kcat.reshape(bd, dec, -1),
                     cache_mla_latent[l], rope_t, w["wuv_pad"], n_pages)
    ob = _moba_sample(page_flat, qm.reshape(bd, dec, -1), km.reshape(bd, dec, -1), vm.reshape(bd, dec, -1),
                      moba_kt, moba_vt, n_pages)
    y_s = _peer(_merge(hs, ya.reshape(n_s, -1), ob.reshape(n_s, -1), w["w_up_b"], w, tm_s), w,
                _token_tile(n_s, 256))
    outs_s = (c.reshape(1, bd, dec, KV_LORA), kr.reshape(1, bd, dec, MLA_ROPE),
              km.reshape(1, bd, dec, MOBA_HEADS, MOBA_DH), vm.reshape(1, bd, dec, MOBA_HEADS, MOBA_DH))
    return (y_p.reshape(batch, seq, D_MODEL), y_s.reshape(bd, dec, D_MODEL), *outs_p, *outs_s)
```

```python
import functools

import numpy as np
import jax
import jax.numpy as jnp
from jax import lax
from jax.experimental import pallas as pl
from jax.experimental.pallas import tpu as pltpu

F32 = jnp.float32
BF16 = jnp.bfloat16

D_MODEL = 1024
PAGE_SIZE = 128

MLA_HEADS = 8
MLA_NOPE = 64
MLA_ROPE = 32
MLA_V = 64
Q_LORA = 768
KV_LORA = 256
ROPE_THETA = 10000.0
MLA_SCALE = (MLA_NOPE + MLA_ROPE) ** -0.5
ROPE_PAD = 128
QCAT = KV_LORA + ROPE_PAD

MOBA_HEADS = 8
MOBA_DH = 64
MOBA_WIDTH = MOBA_HEADS * MOBA_DH
MOBA_BLOCK = 256
MOBA_TOPK = 3
MOBA_SCALE = MOBA_DH ** -0.5
MOBA_PAD = 128
MOBA_PADW = MOBA_HEADS * MOBA_PAD

PEER_HEADS = 8
PEER_NKEYS = 128
PEER_DKEY = 256
PEER_TOPK = 16
PEER_ECHUNK = 4096
PEER_ICHUNK = PEER_ECHUNK // PEER_NKEYS
PEER_KTILE = 256
PEER_NCAND = 72

RMS_EPS = 1e-6
NEG_INF = -1e30

VMEM_LIMIT_BYTES = 56 * 1024 * 1024


def _cparams(*sem):
    return pltpu.CompilerParams(dimension_semantics=sem, vmem_limit_bytes=VMEM_LIMIT_BYTES)


def _rms(xf, w):
    return xf * lax.rsqrt(jnp.mean(xf * xf, axis=-1, keepdims=True) + RMS_EPS) * w


def _dot_nt(a, b):
    return lax.dot_general(a, b, (((1,), (1,)), ((), ())), preferred_element_type=F32)


def _full_spec(shape):
    n = len(shape)
    return pl.BlockSpec(shape, lambda *_: (0,) * n)


_A_QA = 0
_A_CKV = Q_LORA
_A_QM = _A_CKV + KV_LORA
_A_KM = _A_QM + MOBA_WIDTH
_A_VM = _A_KM + MOBA_WIDTH
_A_KR = _A_VM + MOBA_WIDTH
_A_KRS = _A_KR + ROPE_PAD
_A_WIDTH = _A_KRS + ROPE_PAD
_Q_NOPE = 0
_Q_ROPE = MLA_HEADS * MLA_NOPE
_Q_ROPES = _Q_ROPE + MLA_HEADS * ROPE_PAD
_Q_WIDTH = _Q_ROPES + MLA_HEADS * ROPE_PAD


def _pad_heads(v, lane_lo, fill):
    out = []
    for p in range(MOBA_HEADS // 2):
        blk = v[:, p * 128:(p + 1) * 128]
        out.append(jnp.where(lane_lo, blk, fill))
        out.append(jnp.where(lane_lo, pltpu.roll(blk, MOBA_DH, axis=1), fill))
    return jnp.concatenate(out, axis=1)


def _inproj_body(x_ref, cs_ref, sn_ref, an_ref, wa_ref, qan_ref, wqb_ref, kvn_ref, wuk_ref,
                 qcat_ref, kcat_ref, c_ref, kr_ref, qm_ref, km_ref, vm_ref, qp_ref, kp_ref, vp_ref):
    xn = _rms(x_ref[...], an_ref[...]).astype(BF16)
    z = jnp.dot(xn, wa_ref[...], preferred_element_type=F32)
    cs = cs_ref[...]
    sn = sn_ref[...]
    qn = _rms(z[:, _A_QA:_A_QA + Q_LORA], qan_ref[...]).astype(BF16)
    q = jnp.dot(qn, wqb_ref[...], preferred_element_type=F32)
    for p in range(MLA_HEADS // 2):
        lat = jnp.dot(q[:, 128 * p:128 * (p + 1)].astype(BF16), wuk_ref[p], preferred_element_type=F32)
        for hl in range(2):
            h = 2 * p + hl
            qcat_ref[:, h * QCAT:h * QCAT + KV_LORA] = (lat[:, hl * KV_LORA:(hl + 1) * KV_LORA] * MLA_SCALE).astype(BF16)
            rope = (q[:, _Q_ROPE + h * ROPE_PAD:_Q_ROPE + (h + 1) * ROPE_PAD] * cs
                    + q[:, _Q_ROPES + h * ROPE_PAD:_Q_ROPES + (h + 1) * ROPE_PAD] * sn)
            qcat_ref[:, h * QCAT + KV_LORA:(h + 1) * QCAT] = (rope * MLA_SCALE).astype(BF16)
    c = _rms(z[:, _A_CKV:_A_CKV + KV_LORA], kvn_ref[...])
    c_ref[...] = c
    kr = z[:, _A_KR:_A_KR + ROPE_PAD] * cs + z[:, _A_KRS:_A_KRS + ROPE_PAD] * sn
    kr_ref[...] = kr[:, :MLA_ROPE]
    kcat_ref[:, :KV_LORA] = c.astype(BF16)
    ones_lane = lax.broadcasted_iota(jnp.int32, kr.shape, 1) == ROPE_PAD - 1
    kcat_ref[:, KV_LORA:] = jnp.where(ones_lane, 1.0, kr).astype(BF16)
    qm = z[:, _A_QM:_A_QM + MOBA_WIDTH]
    km = z[:, _A_KM:_A_KM + MOBA_WIDTH]
    vm = z[:, _A_VM:_A_VM + MOBA_WIDTH]
    qm_ref[...] = qm
    km_ref[...] = km
    vm_ref[...] = vm
    lane = lax.broadcasted_iota(jnp.int32, (qm.shape[0], 128), 1)
    lane_lo = lane < MOBA_DH
    zeros = jnp.zeros(lane.shape, F32)
    qp_ref[...] = _pad_heads(qm * MOBA_SCALE, lane_lo, zeros).astype(BF16)
    kp_ref[...] = _pad_heads(km, lane_lo, zeros).astype(BF16)
    vp_ref[...] = _pad_heads(vm, lane_lo, jnp.where(lane == MOBA_PAD - 1, 1.0, 0.0)).astype(BF16)


def _inproj(x2, cs, sn, w, tm):
    n = x2.shape[0]
    tab_blocks = cs.shape[0] // tm
    row = lambda i: (i, 0)
    tab = lambda i: (i % tab_blocks, 0)
    widths = [(MLA_HEADS * QCAT, BF16), (QCAT, BF16), (KV_LORA, F32), (MLA_ROPE, F32),
              (MOBA_WIDTH, F32), (MOBA_WIDTH, F32), (MOBA_WIDTH, F32),
              (MOBA_PADW, BF16), (MOBA_PADW, BF16), (MOBA_PADW, BF16)]
    return pl.pallas_call(
        _inproj_body,
        grid=(n // tm,),
        in_specs=[pl.BlockSpec((tm, D_MODEL), row),
                  pl.BlockSpec((tm, ROPE_PAD), tab), pl.BlockSpec((tm, ROPE_PAD), tab),
                  _full_spec((1, D_MODEL)), _full_spec((D_MODEL, _A_WIDTH)),
                  _full_spec((1, Q_LORA)), _full_spec((Q_LORA, _Q_WIDTH)),
                  _full_spec((1, KV_LORA)), _full_spec((MLA_HEADS // 2, 128, 2 * KV_LORA))],
        out_specs=[pl.BlockSpec((tm, wd), row) for wd, _ in widths],
        out_shape=[jax.ShapeDtypeStruct((n, wd), dt) for wd, dt in widths],
        compiler_params=_cparams("arbitrary"),
        name="inproj",
    )(x2, cs, sn, w["attn_norm"], w["w_a"], w["q_a_norm"], w["w_qb"], w["kv_a_norm"], w["w_uk_pairs"])


def _softmax_step(s, v, m_scr, l_scr, acc_scr):
    m_prev = m_scr[...]
    m_new = jnp.maximum(m_prev, jnp.max(s, axis=-1, keepdims=True))
    alpha = jnp.exp(m_prev - m_new)
    p = jnp.exp(s - m_new)
    l_scr[...] = alpha * l_scr[...] + jnp.sum(p, axis=-1, keepdims=True)
    acc_scr[...] = alpha * acc_scr[...] + jnp.dot(p.astype(BF16), v, preferred_element_type=F32)
    m_scr[...] = m_new


def _mla_prompt_body(qcat_ref, kcat_ref, wuv_ref, out_ref, m_scr, acc_scr, *, tq):
    qi = pl.program_id(1)
    m_scr[...] = jnp.full(m_scr.shape, NEG_INF, F32)
    acc_scr[...] = jnp.zeros(acc_scr.shape, F32)

    def chunk(kc, masked):
        k = kcat_ref[pl.ds(pl.multiple_of(kc * tq, tq), tq), :]
        for h in range(MLA_HEADS):
            s = _dot_nt(qcat_ref[:, h * QCAT:(h + 1) * QCAT], k)
            if masked:
                tok = lax.broadcasted_iota(jnp.int32, s.shape, 0)
                col = lax.broadcasted_iota(jnp.int32, s.shape, 1)
                s = jnp.where(col <= tok, s, NEG_INF)
            m_prev = m_scr[h]
            m_new = jnp.maximum(m_prev, jnp.max(s, axis=-1, keepdims=True))
            p = jnp.exp(s - m_new).astype(BF16)
            acc_scr[h] = jnp.exp(m_prev - m_new) * acc_scr[h] + jnp.dot(p, k, preferred_element_type=F32)
            m_scr[h] = m_new

    def past(kc, carry):
        chunk(kc, False)
        return carry

    lax.fori_loop(0, qi, past, 0)
    chunk(qi, True)
    y = jnp.zeros(out_ref.shape, F32)
    for h in range(MLA_HEADS):
        acc = acc_scr[h]
        o = (acc[:, :KV_LORA] / acc[:, QCAT - 1:QCAT]).astype(BF16)
        y = y + jnp.dot(o, wuv_ref[h], preferred_element_type=F32)
    out_ref[...] = y.astype(out_ref.dtype)


def _mla_prompt(qcat, kcat, wuv_pad, batch, seq, tq):
    nq = seq // tq
    return pl.pallas_call(
        functools.partial(_mla_prompt_body, tq=tq),
        grid=(batch, nq),
        in_specs=[pl.BlockSpec((tq, MLA_HEADS * QCAT), lambda b, i: (b * nq + i, 0)),
                  pl.BlockSpec((seq, QCAT), lambda b, i: (b, 0)),
                  _full_spec((MLA_HEADS, KV_LORA, MLA_HEADS * MLA_V))],
        out_specs=pl.BlockSpec((tq, MLA_HEADS * MLA_V), lambda b, i: (b * nq + i, 0)),
        out_shape=jax.ShapeDtypeStruct((batch * seq, MLA_HEADS * MLA_V), BF16),
        scratch_shapes=[pltpu.VMEM((MLA_HEADS, tq, 1), F32), pltpu.VMEM((MLA_HEADS, tq, QCAT), F32)],
        compiler_params=_cparams("arbitrary", "arbitrary"),
        name="mla_prompt",
    )(qcat, kcat, wuv_pad)


def _alibi_slope(h):
    return 2.0 ** (-8.0 * (h + 1) / MOBA_HEADS)


def _block_choice(scores, n_valid):
    lane = lax.broadcasted_iota(jnp.int32, scores.shape, 1)
    n = lane & 7
    s = jnp.where(n < n_valid, scores, NEG_INF)
    beaten = jnp.zeros(scores.shape, F32)
    for d in range(1, 8):
        other = jnp.where(n >= d, pltpu.roll(s, d, axis=1), pltpu.roll(s, 128 - 8 + d, axis=1))
        wins = (other > s) | ((other == s) & (n >= d))
        beaten = beaten + wins.astype(F32)
    return jnp.where((beaten < MOBA_TOPK) & (n < n_valid), 1.0, 0.0)


def _moba_prompt_body(qf_ref, kf_ref, qp_ref, kp_ref, vp_ref, out_ref, kmean_scr, sel_scr, m_scr, acc_scr):
    qi = pl.program_id(1)
    nblk = kmean_scr.shape[0]
    blk = MOBA_BLOCK

    @pl.when(qi == 0)
    def _():
        kmean_scr[...] = jnp.zeros(kmean_scr.shape, F32)
        for n in range(kf_ref.shape[0] // blk):
            kmean_scr[n:n + 1, :] = jnp.mean(kf_ref[n * blk:(n + 1) * blk, :], axis=0, keepdims=True)

    m_scr[...] = jnp.full(m_scr.shape, NEG_INF, F32)
    acc_scr[...] = jnp.zeros(acc_scr.shape, F32)
    km = kmean_scr[...]
    tiled = jnp.concatenate([km] * MOBA_HEADS + [jnp.zeros((128 - MOBA_HEADS * nblk, MOBA_WIDTH), F32)], axis=0)
    row_h = lax.broadcasted_iota(jnp.int32, tiled.shape, 0) // nblk
    lane_h = lax.broadcasted_iota(jnp.int32, tiled.shape, 1) // MOBA_DH
    kmbd = jnp.where(row_h == lane_h, tiled, 0.0)
    scores = lax.dot_general(qf_ref[...], kmbd, (((1,), (1,)), ((), ())),
                             preferred_element_type=F32, precision=lax.Precision.HIGHEST)
    sel_scr[...] = _block_choice(scores, qi).astype(BF16)

    def attend(j, own):
        rows = pl.ds(pl.multiple_of(j * blk, blk), blk)
        r = lax.broadcasted_iota(jnp.int32, (blk, blk), 0)
        c = lax.broadcasted_iota(jnp.int32, (blk, blk), 1)
        dist = ((qi - j) * blk + (r - c)).astype(F32)
        if not own:
            er = lax.broadcasted_iota(jnp.int32, (128, MOBA_PADW), 0)
            ec = lax.broadcasted_iota(jnp.int32, (128, MOBA_PADW), 1)
            spread = jnp.where(((er >> 3) == (ec >> 7)) & ((er & 7) == j), 1.0, 0.0).astype(BF16)
            chosen = jnp.dot(sel_scr[...], spread, preferred_element_type=F32)
        for h in range(MOBA_HEADS):
            sl = slice(h * MOBA_PAD, (h + 1) * MOBA_PAD)
            s = _dot_nt(qp_ref[:, sl], kp_ref[rows, sl])
            s = s - _alibi_slope(h) * dist
            if own:
                s = jnp.where(r >= c, s, NEG_INF)
            else:
                s = jnp.where(jnp.concatenate([chosen[:, sl]] * (blk // MOBA_PAD), axis=1) > 0.5, s, NEG_INF)
            m_prev = m_scr[h]
            m_new = jnp.maximum(m_prev, jnp.max(s, axis=-1, keepdims=True))
            p = jnp.exp(s - m_new).astype(BF16)
            acc_scr[h] = (jnp.exp(m_prev - m_new) * acc_scr[h]
                          + jnp.dot(p, vp_ref[rows, sl], preferred_element_type=F32))
            m_scr[h] = m_new

    def past(j, carry):
        attend(j, False)
        return carry

    lax.fori_loop(0, qi, past, 0)
    attend(qi, True)
    head_lane = lax.broadcasted_iota(jnp.int32, (blk, MOBA_PAD), 1) < MOBA_DH
    for h in range(MOBA_HEADS):
        acc = acc_scr[h]
        o = jnp.where(head_lane, acc / acc[:, MOBA_PAD - 1:MOBA_PAD], 0.0)
        out_ref[:, h * MOBA_PAD:(h + 1) * MOBA_PAD] = o.astype(out_ref.dtype)


def _moba_prompt(qm, km, qp, kp, vp, batch, seq):
    nb = seq // MOBA_BLOCK
    assert nb <= 8 and seq % MOBA_BLOCK == 0
    qmap = lambda b, i: (b * nb + i, 0)
    bmap = lambda b, i: (b, 0)
    blk = MOBA_BLOCK
    return pl.pallas_call(
        _moba_prompt_body,
        grid=(batch, nb),
        in_specs=[pl.BlockSpec((blk, MOBA_WIDTH), qmap), pl.BlockSpec((seq, MOBA_WIDTH), bmap),
                  pl.BlockSpec((blk, MOBA_PADW), qmap),
                  pl.BlockSpec((seq, MOBA_PADW), bmap), pl.BlockSpec((seq, MOBA_PADW), bmap)],
        out_specs=pl.BlockSpec((blk, MOBA_PADW), qmap),
        out_shape=jax.ShapeDtypeStruct((batch * seq, MOBA_PADW), BF16),
        scratch_shapes=[pltpu.VMEM((8, MOBA_WIDTH), F32), pltpu.VMEM((blk, 128), BF16),
                        pltpu.VMEM((MOBA_HEADS, blk, 1), F32), pltpu.VMEM((MOBA_HEADS, blk, MOBA_PAD), F32)],
        compiler_params=_cparams("arbitrary", "arbitrary"),
        name="moba_prompt",
    )(qm, km, qp, kp, vp)


def _sigmoid(g):
    return 1.0 / (1.0 + jnp.exp(-g))


def _merge_body(x_ref, ya_ref, ob_ref, an_ref, wg_ref, wua_ref, wub_ref, wo_ref, h_ref):
    x = x_ref[...]
    xn = _rms(x, an_ref[...]).astype(BF16)
    g = jnp.dot(xn, wg_ref[...], preferred_element_type=F32)
    y_a = jnp.dot(ya_ref[...], wua_ref[...], preferred_element_type=F32)
    y_b = jnp.dot(ob_ref[...], wub_ref[...], preferred_element_type=F32)
    mix = _sigmoid(g[:, :D_MODEL]) * y_a + _sigmoid(g[:, D_MODEL:]) * y_b
    h_ref[...] = x + jnp.dot(mix.astype(BF16), wo_ref[...], preferred_element_type=F32)


def _merge(x2, ya, ob, w_up_b, w, tm):
    n = x2.shape[0]
    wb = ob.shape[1]
    row = lambda i: (i, 0)
    return pl.pallas_call(
        _merge_body,
        grid=(n // tm,),
        in_specs=[pl.BlockSpec((tm, D_MODEL), row), pl.BlockSpec((tm, MLA_HEADS * MLA_V), row),
                  pl.BlockSpec((tm, wb), row),
                  _full_spec((1, D_MODEL)), _full_spec((D_MODEL, 2 * D_MODEL)),
                  _full_spec((MLA_HEADS * MLA_V, D_MODEL)), _full_spec((wb, D_MODEL)),
                  _full_spec((D_MODEL, D_MODEL))],
        out_specs=pl.BlockSpec((tm, D_MODEL), row),
        out_shape=jax.ShapeDtypeStruct((n, D_MODEL), F32),
        compiler_params=_cparams("arbitrary"),
        name="merge",
    )(x2, ya, ob, w["attn_norm"], w["w_g"], w["w_up_a"], w_up_b, w["w_o"])


def _topk_extract(s, key):
    rank = jnp.full(s.shape, float(PEER_TOPK), F32)
    work = s
    vals = []
    for r in range(PEER_TOPK):
        m = jnp.max(work, axis=0, keepdims=True)
        if key is None:
            hit = work == m
        else:
            hit = key == jnp.min(jnp.where(work == m, key, 1e9), axis=0, keepdims=True)
        rank = jnp.where(hit, float(r), rank)
        work = jnp.where(hit, -jnp.inf, work)
        vals.append(m)
    return rank, vals


def _topk_ranks(s, key, rank_scr, vals_scr):
    rank, vals = _topk_extract(s, None)
    rank_scr[...] = rank
    vals_scr[...] = jnp.concatenate(vals, axis=0)
    marked = jnp.sum(jnp.where(rank < float(PEER_TOPK), 1.0, 0.0))

    @pl.when(marked != float(PEER_TOPK * s.shape[1]))
    def _():
        rank_k, vals_k = _topk_extract(s, key)
        rank_scr[...] = rank_k
        vals_scr[...] = jnp.concatenate(vals_k, axis=0)

    return rank_scr[...], [vals_scr[r:r + 1, :] for r in range(PEER_TOPK)]


def _gelu(x):
    return 0.5 * x * (1.0 + lax.erf(x * np.float32(np.sqrt(0.5))))


def _peer_body(h_ref, fn_ref, wq_ref, k1_ref, k2_ref, u_ref, vt_ref, on_ref, y_ref,
               xn_scr, qt_scr, r2_scr, e2_scr, nb_scr, p1_scr, acc_scr,
               rk1_scr, rk2_scr, rkc_scr, v1_scr, v2_scr, vc_scr):
    c = pl.program_id(1)
    t = h_ref.shape[0]
    kk = PEER_TOPK

    @pl.when(c == 0)
    def _():
        xn = _rms(h_ref[...], fn_ref[...]).astype(BF16)
        xn_scr[...] = xn
        qt_scr[...] = _dot_nt(wq_ref[...], xn)
        acc_scr[...] = jnp.zeros(acc_scr.shape, F32)
        key128 = lax.broadcasted_iota(jnp.int32, (PEER_NKEYS, t), 0).astype(F32)
        r16 = lax.broadcasted_iota(jnp.int32, (kk, t), 0).astype(F32)
        r8 = lax.broadcasted_iota(jnp.int32, (8, t), 0).astype(F32)

        def per_head(hd, carry):
            base = pl.multiple_of(hd * PEER_DKEY, PEER_DKEY)
            q1 = qt_scr[pl.ds(base, PEER_DKEY // 2), :].astype(BF16)
            q2 = qt_scr[pl.ds(base + PEER_DKEY // 2, PEER_DKEY // 2), :].astype(BF16)
            s1 = jnp.dot(k1_ref[hd], q1, preferred_element_type=F32)
            s2 = jnp.dot(k2_ref[hd], q2, preferred_element_type=F32)
            rank1, v1r = _topk_ranks(s1, key128, rk1_scr, v1_scr)
            rank2, v2r = _topk_ranks(s2, key128, rk2_scr, v2_scr)
            v1, v1h = jnp.concatenate(v1r, axis=0), jnp.concatenate(v1r[:8], axis=0)
            v2, v2h = jnp.concatenate(v2r, axis=0), jnp.concatenate(v2r[:8], axis=0)
            slabs = [(v1 + v2r[0], r16 * kk, None),
                     (v1h + v2r[1], r8 * kk + 1.0, None),
                     (v2 + v1r[0], r16, r16 >= 2.0),
                     (v2h + v1r[1], r8 + float(kk), r8 >= 2.0)]
            for b in (2, 3, 4):
                slabs.append((v1h + v2r[b], r8 * kk + float(b), (r8 >= 2.0) & ((r8 + 1.0) * (b + 1) <= kk)))
            cand = jnp.concatenate([vals if ok is None else jnp.where(ok, vals, -jnp.inf) for vals, _, ok in slabs],
                                   axis=0)
            ckey = jnp.concatenate([key if ok is None else jnp.where(ok, key, 1e6) for _, key, ok in slabs], axis=0)
            rankc, _ = _topk_ranks(cand, ckey, rkc_scr, vc_scr)
            selc = jnp.where(rankc < float(kk), 1.0, 0.0)
            sel_a, sel_b, sel_c, sel_d = selc[0:16], selc[16:24], selc[24:40], selc[40:48]
            sel_e = [selc[48:56], selc[56:64], selc[64:72]]
            e1, e1h = jnp.exp(v1 - v1r[0]), jnp.exp(v1h - v1r[0])
            e2, e2h = jnp.exp(v2 - v2r[0]), jnp.exp(v2h - v2r[0])
            e1r = [jnp.exp(v1r[a] - v1r[0]) for a in range(kk)]
            e2r = [jnp.exp(v2r[b] - v2r[0]) for b in range(kk)]
            colsum = lambda x: jnp.sum(x, axis=0, keepdims=True)
            zero8 = jnp.zeros((8, t), F32)
            nb = (sel_a + jnp.concatenate([sel_b + sel_e[0] + sel_e[1] + sel_e[2], zero8], axis=0)
                  + jnp.where(r16 == 0.0, colsum(sel_c), 0.0) + jnp.where(r16 == 1.0, colsum(sel_d), 0.0))
            zsum = (colsum(sel_a * e1) * e2r[0] + colsum(sel_b * e1h) * e2r[1]
                    + colsum(sel_c * e2) * e1r[0] + colsum(sel_d * e2h) * e1r[1])
            for n, b in enumerate((2, 3, 4)):
                zsum = zsum + colsum(sel_e[n] * e1h) * e2r[b]
            p1x = jnp.where(rank1 < float(kk), jnp.exp(s1 - v1r[0]), 0.0)
            e2x = jnp.where(rank2 < float(kk), jnp.exp(s2 - v2r[0]) / zsum, 0.0)
            nbx = jnp.zeros((PEER_NKEYS, t), F32)
            for a in range(kk):
                nbx = jnp.where(rank1 == float(a), nb[a:a + 1], nbx)
            r2_scr[hd] = rank2.astype(BF16)
            e2_scr[hd] = e2x.astype(BF16)
            nb_scr[hd] = nbx
            p1_scr[hd] = p1x
            return carry

        lax.fori_loop(0, PEER_HEADS, per_head, 0)

    xn = xn_scr[...]
    acc = None
    for kt in range(PEER_ECHUNK // PEER_KTILE):
        rows = slice(kt * PEER_KTILE, (kt + 1) * PEER_KTILE)
        act = _dot_nt(u_ref[rows, :], xn)
        parts = []
        for ii in range(PEER_KTILE // PEER_NKEYS):
            i = c * PEER_ICHUNK + kt * (PEER_KTILE // PEER_NKEYS) + ii
            wgt = jnp.zeros((PEER_NKEYS, t), BF16)
            for hd in range(PEER_HEADS):
                nbrow = nb_scr[hd, pl.ds(i, 1), :].astype(BF16)
                p1row = p1_scr[hd, pl.ds(i, 1), :].astype(BF16)
                wgt = wgt + jnp.where(r2_scr[hd] < nbrow, e2_scr[hd] * p1row, jnp.zeros((), BF16))
            parts.append(wgt * _gelu(act[ii * PEER_NKEYS:(ii + 1) * PEER_NKEYS]).astype(BF16))
        piece = jnp.dot(vt_ref[:, rows], jnp.concatenate(parts, axis=0), preferred_element_type=F32)
        acc = piece if acc is None else acc + piece
    acc_scr[...] += acc

    @pl.when(c == pl.num_programs(1) - 1)
    def _():
        h2 = h_ref[...] + acc_scr[...].T
        y_ref[...] = _rms(h2, on_ref[...])


def _peer(h2d, w, t):
    n = h2d.shape[0]
    n_exp = w["peer_u"].shape[0]
    nchunk = n_exp // PEER_ECHUNK
    slab = (PEER_HEADS, PEER_NKEYS, t)
    return pl.pallas_call(
        _peer_body,
        grid=(n // t, nchunk),
        in_specs=[pl.BlockSpec((t, D_MODEL), lambda i, c: (i, 0)),
                  _full_spec((1, D_MODEL)), _full_spec((PEER_HEADS * PEER_DKEY, D_MODEL)),
                  _full_spec((PEER_HEADS, PEER_NKEYS, PEER_DKEY // 2)),
                  _full_spec((PEER_HEADS, PEER_NKEYS, PEER_DKEY // 2)),
                  pl.BlockSpec((PEER_ECHUNK, D_MODEL), lambda i, c: (c, 0)),
                  pl.BlockSpec((D_MODEL, PEER_ECHUNK), lambda i, c: (0, c)),
                  _full_spec((1, D_MODEL))],
        out_specs=pl.BlockSpec((t, D_MODEL), lambda i, c: (i, 0)),
        out_shape=jax.ShapeDtypeStruct((n, D_MODEL), F32),
        scratch_shapes=[pltpu.VMEM((t, D_MODEL), BF16), pltpu.VMEM((PEER_HEADS * PEER_DKEY, t), F32),
                        pltpu.VMEM(slab, BF16), pltpu.VMEM(slab, BF16), pltpu.VMEM(slab, F32), pltpu.VMEM(slab, F32),
                        pltpu.VMEM((D_MODEL, t), F32),
                        pltpu.VMEM((PEER_NKEYS, t), F32), pltpu.VMEM((PEER_NKEYS, t), F32),
                        pltpu.VMEM((PEER_NCAND, t), F32)] + [pltpu.VMEM((PEER_TOPK, t), F32)] * 3,
        compiler_params=_cparams("arbitrary", "arbitrary"),
        name="peer",
    )(h2d, w["ffn_norm"], w["peer_wq_t"], w["peer_k1"], w["peer_k2"], w["peer_u"], w["peer_vt"], w["final_norm"])


SAMPLE_PAGES_PER_STEP = 32
MLA_PAGES_PER_STEP = 64


def _mla_sample_body(pt_ref, q_ref, knew_ref, *rest, pages, dec):
    c_refs = rest[:pages]
    r_refs = rest[pages:2 * pages]
    wuv_ref, out_ref, q_scr, knew_scr, m_scr, l_scr, acc_scr = rest[2 * pages:]
    s_id = pl.program_id(1)
    rows = MLA_HEADS * dec

    @pl.when(s_id == 0)
    def _():
        for h in range(MLA_HEADS):
            q_scr[h * dec:(h + 1) * dec, :] = q_ref[0, :, h * QCAT:(h + 1) * QCAT].astype(F32)
        m_scr[...] = jnp.full(m_scr.shape, NEG_INF, F32)
        l_scr[...] = jnp.zeros(l_scr.shape, F32)
        acc_scr[...] = jnp.zeros(acc_scr.shape, F32)

    q = q_scr[...].astype(BF16)
    q_lat = q[:, :KV_LORA]
    q_rope = q[:, KV_LORA:KV_LORA + MLA_ROPE]
    cs = [c_refs[i][0].astype(BF16) for i in range(pages)]
    s = jnp.concatenate([_dot_nt(q_lat, cs[i])
                         + jnp.dot(q_rope, r_refs[i][0].astype(BF16), preferred_element_type=F32)
                         for i in range(pages)], axis=1)
    m_prev = m_scr[...]
    m_new = jnp.maximum(m_prev, jnp.max(s, axis=-1, keepdims=True))
    alpha = jnp.exp(m_prev - m_new)
    p = jnp.exp(s - m_new)
    l_scr[...] = alpha * l_scr[...] + jnp.sum(p, axis=-1, keepdims=True)
    p = p.astype(BF16)
    pv = jnp.zeros(acc_scr.shape, F32)
    for i in range(pages):
        pv = pv + jnp.dot(p[:, i * PAGE_SIZE:(i + 1) * PAGE_SIZE], cs[i], preferred_element_type=F32)
    acc_scr[...] = alpha * acc_scr[...] + pv
    m_scr[...] = m_new

    @pl.when(s_id == pl.num_programs(1) - 1)
    def _():
        knew_scr[...] = jnp.zeros(knew_scr.shape, F32)
        knew_scr[0:dec, :] = knew_ref[0].astype(F32)
        k = knew_scr[...].astype(BF16)
        s_new = _dot_nt(q, k)
        tok = lax.broadcasted_iota(jnp.int32, s_new.shape, 0) & (dec - 1)
        col = lax.broadcasted_iota(jnp.int32, s_new.shape, 1)
        s_new = jnp.where(col <= tok, s_new, NEG_INF)
        _softmax_step(s_new, k[:, :KV_LORA], m_scr, l_scr, acc_scr)
        o = (acc_scr[...] / l_scr[...]).astype(BF16)
        y = jnp.zeros((dec, MLA_HEADS * MLA_V), F32)
        for h in range(MLA_HEADS):
            y = y + jnp.dot(o[h * dec:(h + 1) * dec], wuv_ref[h], preferred_element_type=F32)
        out_ref[0] = y.astype(out_ref.dtype)


def _mla_sample(page_flat, qcat3, kcat3, cache_c, cache_r, wuv_pad, n_pages):
    bd, dec, _ = qcat3.shape
    assert dec & (dec - 1) == 0 and dec <= PAGE_SIZE
    pages = min(MLA_PAGES_PER_STEP, n_pages)
    steps = n_pages // pages
    rows = MLA_HEADS * dec

    def page_map(i):
        return lambda b, s, pt: (pt[b * n_pages + s * pages + i], 0, 0)

    per_b = lambda b, s, pt: (b, 0, 0)
    grid_spec = pltpu.PrefetchScalarGridSpec(
        num_scalar_prefetch=1,
        grid=(bd, steps),
        in_specs=([pl.BlockSpec((1, dec, MLA_HEADS * QCAT), per_b), pl.BlockSpec((1, dec, QCAT), per_b)]
                  + [pl.BlockSpec((1, PAGE_SIZE, KV_LORA), page_map(i)) for i in range(pages)]
                  + [pl.BlockSpec((1, MLA_ROPE, PAGE_SIZE), page_map(i)) for i in range(pages)]
                  + [pl.BlockSpec((MLA_HEADS, KV_LORA, MLA_HEADS * MLA_V), lambda b, s, pt: (0, 0, 0))]),
        out_specs=pl.BlockSpec((1, dec, MLA_HEADS * MLA_V), per_b),
        scratch_shapes=[pltpu.VMEM((rows, QCAT), F32), pltpu.VMEM((PAGE_SIZE, QCAT), F32),
                        pltpu.VMEM((rows, 1), F32), pltpu.VMEM((rows, 1), F32), pltpu.VMEM((rows, KV_LORA), F32)],
    )
    return pl.pallas_call(
        functools.partial(_mla_sample_body, pages=pages, dec=dec),
        grid_spec=grid_spec,
        out_shape=jax.ShapeDtypeStruct((bd, dec, MLA_HEADS * MLA_V), BF16),
        compiler_params=_cparams("arbitrary", "arbitrary"),
        name="mla_sample",
    )(page_flat, qcat3, kcat3, *([cache_c] * pages), *([cache_r] * pages), wuv_pad)


def _group_choice(scores, group, n_valid, topk):
    lane = lax.broadcasted_iota(jnp.int32, scores.shape, 1)
    s = jnp.where(lane < n_valid, scores, NEG_INF)
    beaten = jnp.zeros(scores.shape, F32)
    for d in range(1, group):
        other = jnp.where(lane >= d, pltpu.roll(s, d, axis=1), pltpu.roll(s, 128 - group + d, axis=1))
        wins = (other > s) | ((other == s) & (lane >= d))
        beaten = beaten + wins.astype(F32)
    return jnp.where((beaten < topk) & (lane < n_valid), 1.0, 0.0)


def _moba_sample_body(pt_ref, qm_ref, kn_ref, vn_ref, e_ref, *rest, pages, dec, n_pages):
    k_refs = rest[:pages]
    v_refs = rest[pages:2 * pages]
    out_ref, qbd_scr, kmt_scr, s_scr, snew_scr, new_scr, l_scr, acc_scr = rest[2 * pages:]
    s_id = pl.program_id(1)
    n_k = n_pages // pages
    rows = MOBA_HEADS * dec
    n_blocks = n_pages // 2
    past = n_pages * PAGE_SIZE
    row = lax.broadcasted_iota(jnp.int32, (rows, PAGE_SIZE), 0)
    col = lax.broadcasted_iota(jnp.int32, (rows, PAGE_SIZE), 1)
    tok = row & (dec - 1)
    slope = jnp.zeros((rows, PAGE_SIZE), F32)
    for h in range(MOBA_HEADS):
        slope = jnp.where(row // dec == h, _alibi_slope(h), slope)

    @pl.when(s_id == 0)
    def _():
        tiled = jnp.concatenate([qm_ref[0]] * MOBA_HEADS, axis=0)
        row_h = lax.broadcasted_iota(jnp.int32, tiled.shape, 0) // dec
        lane_h = lax.broadcasted_iota(jnp.int32, tiled.shape, 1) // MOBA_DH
        qbd_scr[...] = jnp.where(row_h == lane_h, tiled * MOBA_SCALE, 0.0)
        kmt_scr[...] = jnp.zeros(kmt_scr.shape, F32)

    @pl.when(s_id < n_k)
    def _():
        qbd = qbd_scr[...].astype(BF16)
        lane = lax.broadcasted_iota(jnp.int32, kmt_scr.shape, 1)
        for i in range(0, pages, 2):
            kt0 = k_refs[i][0]
            kt1 = k_refs[i + 1][0]
            page = s_id * pages + i
            s_scr[page] = jnp.dot(qbd, kt0.astype(BF16), preferred_element_type=F32)
            s_scr[page + 1] = jnp.dot(qbd, kt1.astype(BF16), preferred_element_type=F32)
            mean = 0.5 * jnp.mean(kt0 + kt1, axis=1, keepdims=True)
            kmt_scr[...] = jnp.where(lane == page // 2, mean, kmt_scr[...])

    @pl.when(s_id == n_k)
    def _():
        scores = jnp.dot(qbd_scr[...], kmt_scr[...], preferred_element_type=F32, precision=lax.Precision.HIGHEST)
        sel = _group_choice(scores, max(n_blocks, 1), n_blocks, MOBA_TOPK)
        chosen = jnp.dot(sel.astype(BF16), e_ref[...], preferred_element_type=F32)
        new_scr[...] = jnp.zeros(new_scr.shape, F32)
        new_scr[0:dec, :] = kn_ref[0]
        s_new = _dot_nt(qbd_scr[...].astype(BF16), new_scr[...].astype(BF16))
        s_new = s_new - slope * (tok - col).astype(F32)
        s_new = jnp.where(col <= tok, s_new, NEG_INF)
        bias = slope * (past + tok - col).astype(F32)
        mx = s_new
        for page in range(n_pages):
            cols = slice(page * PAGE_SIZE, (page + 1) * PAGE_SIZE)
            sm = jnp.where(chosen[:, cols] > 0.5, s_scr[page] - (bias - slope * float(page * PAGE_SIZE)), NEG_INF)
            s_scr[page] = sm
            mx = jnp.maximum(mx, sm)
        m = jnp.max(mx, axis=-1, keepdims=True)
        p_new = jnp.exp(s_new - m)
        snew_scr[...] = p_new
        lsum = p_new
        for page in range(n_pages):
            p = jnp.exp(s_scr[page] - m)
            s_scr[page] = p
            lsum = lsum + p
        l_scr[...] = jnp.sum(lsum, axis=-1, keepdims=True)
        acc_scr[...] = jnp.zeros(acc_scr.shape, F32)

    @pl.when(s_id >= n_k)
    def _():
        pv = jnp.zeros(acc_scr.shape, F32)
        for i in range(pages):
            page = (s_id - n_k) * pages + i
            pv = pv + _dot_nt(s_scr[page].astype(BF16), v_refs[i][0].astype(BF16))
        acc_scr[...] += pv

    @pl.when(s_id == 2 * n_k - 1)
    def _():
        new_scr[...] = jnp.zeros(new_scr.shape, F32)
        new_scr[0:dec, :] = vn_ref[0]
        acc = acc_scr[...] + jnp.dot(snew_scr[...].astype(BF16), new_scr[...].astype(BF16),
                                     preferred_element_type=F32)
        o = acc / l_scr[...]
        row_h = lax.broadcasted_iota(jnp.int32, o.shape, 0) // dec
        lane_h = lax.broadcasted_iota(jnp.int32, o.shape, 1) // MOBA_DH
        o = jnp.where(row_h == lane_h, o, 0.0)
        y = o[0:dec]
        for h in range(1, MOBA_HEADS):
            y = y + o[h * dec:(h + 1) * dec]
        out_ref[0] = y.astype(out_ref.dtype)


def _moba_sample(page_flat, qm3, km3, vm3, cache_kt, cache_vt, n_pages):
    bd, dec, _ = qm3.shape
    assert dec == 8, "row groups of the stacked heads must be one sublane tile"
    assert n_pages % 2 == 0, "the past must be a whole number of MoBA blocks"
    n_blocks = n_pages // 2
    assert n_blocks & (n_blocks - 1) == 0 and n_blocks <= 128
    pages = min(SAMPLE_PAGES_PER_STEP, n_pages)
    n_k = n_pages // pages
    rows = MOBA_HEADS * dec

    def k_map(i):
        return lambda b, s, pt: (pt[b * n_pages + jnp.minimum(s, n_k - 1) * pages + i], 0, 0)

    def v_map(i):
        return lambda b, s, pt: (pt[b * n_pages + jnp.maximum(s - n_k, 0) * pages + i], 0, 0)

    per_b = lambda b, s, pt: (b, 0, 0)
    n_keys = n_pages * PAGE_SIZE
    member = (jnp.arange(n_keys)[None, :] // MOBA_BLOCK == jnp.arange(128)[:, None]).astype(BF16)
    grid_spec = pltpu.PrefetchScalarGridSpec(
        num_scalar_prefetch=1,
        grid=(bd, 2 * n_k),
        in_specs=([pl.BlockSpec((1, dec, MOBA_WIDTH), per_b)] * 3
                  + [pl.BlockSpec((128, n_keys), lambda b, s, pt: (0, 0))]
                  + [pl.BlockSpec((1, MOBA_WIDTH, PAGE_SIZE), k_map(i)) for i in range(pages)]
                  + [pl.BlockSpec((1, MOBA_WIDTH, PAGE_SIZE), v_map(i)) for i in range(pages)]),
        out_specs=pl.BlockSpec((1, dec, MOBA_WIDTH), per_b),
        scratch_shapes=[pltpu.VMEM((rows, MOBA_WIDTH), F32), pltpu.VMEM((MOBA_WIDTH, 128), F32),
                        pltpu.VMEM((n_pages, rows, PAGE_SIZE), F32), pltpu.VMEM((rows, PAGE_SIZE), F32),
                        pltpu.VMEM((PAGE_SIZE, MOBA_WIDTH), F32),
                        pltpu.VMEM((rows, 1), F32), pltpu.VMEM((rows, MOBA_WIDTH), F32)],
    )
    return pl.pallas_call(
        functools.partial(_moba_sample_body, pages=pages, dec=dec, n_pages=n_pages),
        grid_spec=grid_spec,
        out_shape=jax.ShapeDtypeStruct((bd, dec, MOBA_WIDTH), BF16),
        compiler_params=_cparams("arbitrary", "arbitrary"),
        name="moba_sample",
    )(page_flat, qm3, km3, vm3, member, *([cache_kt] * pages), *([cache_vt] * pages))


def _rope_tables(pos):
    half = MLA_ROPE // 2
    inv = ROPE_THETA ** (-jnp.arange(half, dtype=F32) / half)
    ang = pos.astype(F32)[:, None] * inv
    cos, sin = jnp.cos(ang), jnp.sin(ang)
    pad = jnp.zeros((pos.shape[0], ROPE_PAD - MLA_ROPE), F32)
    return jnp.concatenate([cos, cos, pad], axis=1), jnp.concatenate([-sin, sin, pad], axis=1)


def _swap_halves(w):
    half = MLA_ROPE // 2
    return jnp.concatenate([w[..., half:], w[..., :half]], axis=-1)


def _layer_weights(l, attn_norm, w_in, q_a_norm, w_q_b, kv_a_norm, w_uk, w_uv, w_up_a, w_up_b, w_o,
                   ffn_norm, peer_w_query, peer_keys_1, peer_keys_2, peer_u, peer_v, final_norm):
    sizes = (Q_LORA, KV_LORA, MLA_ROPE, MOBA_WIDTH, MOBA_WIDTH, MOBA_WIDTH, D_MODEL, D_MODEL)
    qa, ckv, kr, qm, km, vm, ga, gb = jnp.split(w_in[l], [int(v) for v in np.cumsum(sizes)[:-1]], axis=1)
    pad = jnp.zeros((D_MODEL, ROPE_PAD - MLA_ROPE), F32)
    w_a = jnp.concatenate([qa, ckv, qm, km, vm, kr, pad, _swap_halves(kr), pad], axis=1)
    wq = w_q_b[l].reshape(Q_LORA, MLA_HEADS, MLA_NOPE + MLA_ROPE)
    nope = wq[:, :, :MLA_NOPE].reshape(Q_LORA, MLA_HEADS * MLA_NOPE)
    rope = wq[:, :, MLA_NOPE:]
    padq = ((0, 0), (0, 0), (0, ROPE_PAD - MLA_ROPE))
    rope_p = jnp.pad(rope, padq).reshape(Q_LORA, MLA_HEADS * ROPE_PAD)
    ropes_p = jnp.pad(_swap_halves(rope), padq).reshape(Q_LORA, MLA_HEADS * ROPE_PAD)
    w_qb = jnp.concatenate([nope, rope_p, ropes_p], axis=1)
    wuk_t = jnp.transpose(w_uk[l], (1, 2, 0))
    z = jnp.zeros((MLA_NOPE, KV_LORA), F32)
    pairs = [jnp.concatenate([jnp.concatenate([wuk_t[2 * p], z], axis=1),
                              jnp.concatenate([z, wuk_t[2 * p + 1]], axis=1)], axis=0)
             for p in range(MLA_HEADS // 2)]
    wuv_t = jnp.transpose(w_uv[l], (1, 0, 2))
    wuv_pad = jnp.stack([jnp.pad(wuv_t[h], ((0, 0), (h * MLA_V, (MLA_HEADS - 1 - h) * MLA_V)))
                         for h in range(MLA_HEADS)])
    wub = w_up_b[l].reshape(MOBA_HEADS, MOBA_DH, D_MODEL)
    wub_pad = jnp.pad(wub, ((0, 0), (0, MOBA_PAD - MOBA_DH), (0, 0))).reshape(MOBA_PADW, D_MODEL)
    return {
        "attn_norm": attn_norm[l][None, :], "w_a": w_a.astype(BF16),
        "q_a_norm": q_a_norm[l][None, :], "w_qb": w_qb.astype(BF16),
        "kv_a_norm": kv_a_norm[l][None, :], "w_uk_pairs": jnp.stack(pairs).astype(BF16),
        "wuv_pad": wuv_pad.astype(BF16),
        "w_g": jnp.concatenate([ga, gb], axis=1).astype(BF16),
        "w_up_a": w_up_a[l].astype(BF16), "w_up_b": w_up_b[l].astype(BF16), "w_up_b_pad": wub_pad.astype(BF16),
        "w_o": w_o[l].astype(BF16),
        "ffn_norm": ffn_norm[l][None, :], "peer_wq_t": peer_w_query[l].T.astype(BF16),
        "peer_k1": peer_keys_1[l].astype(BF16), "peer_k2": peer_keys_2[l].astype(BF16),
        "peer_u": peer_u[l].astype(BF16), "peer_vt": peer_v[l].T.astype(BF16),
        "final_norm": final_norm[None, :],
    }


def _token_tile(n, want):
    t = min(want, n)
    assert n % t == 0
    return t


def kernel(x_prompt, x_sample, cache_mla_latent, cache_mla_rope, cache_moba_k, cache_moba_v, page_table,
           attn_norm, w_in, q_a_norm, w_q_b, kv_a_norm, w_uk, w_uv, w_up_a, w_up_b, w_o,
           ffn_norm, peer_w_query, peer_keys_1, peer_keys_2, peer_u, peer_v, final_norm):
    batch, seq, _ = x_prompt.shape
    bd, dec, _ = x_sample.shape
    depth = w_in.shape[0]
    n_pages = page_table.shape[1]
    past = n_pages * PAGE_SIZE
    n_p, n_s = batch * seq, bd * dec
    tm_p = _token_tile(seq, 256)
    tm_s = _token_tile(n_s, 256)
    assert tm_s % dec == 0
    cs_p, sn_p = _rope_tables(jnp.arange(seq))
    cs_s, sn_s = _rope_tables(past + jnp.arange(dec))
    cs_s, sn_s = jnp.tile(cs_s, (tm_s // dec, 1)), jnp.tile(sn_s, (tm_s // dec, 1))
    page_flat = page_table.reshape(-1).astype(jnp.int32)

    assert depth == 1, "the PEER kernel applies the final norm, so it closes the only layer"
    l = 0
    w = _layer_weights(l, attn_norm, w_in, q_a_norm, w_q_b, kv_a_norm, w_uk, w_uv, w_up_a, w_up_b, w_o,
                       ffn_norm, peer_w_query, peer_keys_1, peer_keys_2, peer_u, peer_v, final_norm)
    hp = x_prompt.reshape(n_p, D_MODEL)
    qcat, kcat, c, kr, qm, km, vm, qp, kp, vp = _inproj(hp, cs_p, sn_p, w, tm_p)
    ya = _mla_prompt(qcat, kcat, w["wuv_pad"], batch, seq, _token_tile(seq, 256))
    ob = _moba_prompt(qm, km, qp, kp, vp, batch, seq)
    y_p = _peer(_merge(hp, ya, ob, w["w_up_b_pad"], w, tm_p), w, _token_tile(n_p, 256))
    outs_p = (c.reshape(1, batch, seq, KV_LORA), kr.reshape(1, batch, seq, MLA_ROPE),
              km.reshape(1, batch, seq, MOBA_HEADS, MOBA_DH), vm.reshape(1, batch, seq, MOBA_HEADS, MOBA_DH))
    hs = x_sample.reshape(n_s, D_MODEL)
    qcat, kcat, c, kr, qm, km, vm, _, _, _ = _inproj(hs, cs_s, sn_s, w, tm_s)
    rope_t = jnp.swapaxes(cache_mla_rope[l], 1, 2)
    moba_kt = jnp.transpose(cache_moba_k[l], (0, 2, 3, 1)).reshape(-1, MOBA_WIDTH, PAGE_SIZE)
    moba_vt = jnp.transpose(cache_moba_v[l], (0, 2, 3, 1)).reshape(-1, MOBA_WIDTH, PAGE_SIZE)
    ya = _mla_sample(page_flat, qcat.reshape(bd, dec, -1), kcat.reshape(bd, dec, -1),
                     cache_mla_latent[l], rope_t, w["wuv_pad"], n_pages)
    ob = _moba_sample(page_flat, qm.reshape(bd, dec, -1), km.reshape(bd, dec, -1), vm.reshape(bd, dec, -1),
                      moba_kt, moba_vt, n_pages)
    y_s = _peer(_merge(hs, ya.reshape(n_s, -1), ob.reshape(n_s, -1), w["w_up_b"], w, tm_s), w,
                _token_tile(n_s, 256))
    outs_s = (c.reshape(1, bd, dec, KV_LORA), kr.reshape(1, bd, dec, MLA_ROPE),
              km.reshape(1, bd, dec, MOBA_HEADS, MOBA_DH), vm.reshape(1, bd, dec, MOBA_HEADS, MOBA_DH))
    return (y_p.reshape(batch, seq, D_MODEL), y_s.reshape(bd, dec, D_MODEL), *outs_p, *outs_s)
```
